```python
import math
import jax, jax.numpy as jnp
from jax import lax
import numpy as np

D_MODEL = 2048
BATCH = 1
SEQ = 8192
DEPTH = 1

CONV_WIDTH = D_MODEL // 2
CONV_GROUPS = 16
CONV_K = 3
RET_HEADS = 8
RET_HEAD_DIM = (D_MODEL - CONV_WIDTH) // RET_HEADS
RET_WIDTH = RET_HEADS * RET_HEAD_DIM
MIX_WIDTH = CONV_WIDTH + RET_WIDTH
IN_COLS = 3 * CONV_WIDTH + 4 * RET_WIDTH
RET_CHUNK = 128
ROPE_THETA = 10000.0
N_EXPERTS = 256
TOP_K = 8
N_GROUPS = 8
TOPK_GROUPS = 4
EXPERT_FF = 512
SHARED_FF = 512
ROUTED_SCALE = 2.5
MOE_BLOCK = 128
NORM_EPS = 1e-6
GN_EPS = 1e-5

kernel_name = "hybrid_conv_retention_moe_adaln"

F32 = jnp.float32


def rms_norm(x, g):
    xf = x.astype(F32)
    r = xf * lax.rsqrt(jnp.mean(xf * xf, axis=-1, keepdims=True) + NORM_EPS)
    return (r * g.astype(F32)).astype(x.dtype)


def swiglu(x, wg, wu, wd):
    return (jax.nn.silu(x @ wg) * (x @ wu)) @ wd


def rope(t, pos):
    half = t.shape[-1] // 2
    inv = ROPE_THETA ** (-jnp.arange(half, dtype=F32) / half)
    ang = pos.astype(F32)[..., None] * inv
    cos = jnp.cos(ang)[:, :, None, :]
    sin = jnp.sin(ang)[:, :, None, :]
    t1, t2 = t[..., :half], t[..., half:]
    return jnp.concatenate([t1 * cos - t2 * sin, t1 * sin + t2 * cos], axis=-1)


def short_conv(cb, cc, cx, conv_w):
    S = cx.shape[1]
    u = cc * cx
    up = jnp.pad(u, ((0, 0), (CONV_K - 1, 0), (0, 0)))
    conv = sum(up[:, j:j + S, :] * conv_w[j] for j in range(CONV_K))
    return cb * conv


def retention(q, k, v, g, pos, gn_w):
    Bsz, S, _ = q.shape
    H, dh, C = RET_HEADS, RET_HEAD_DIM, RET_CHUNK
    N = S // C
    q = rope(q.astype(F32).reshape(Bsz, S, H, dh), pos)
    k = rope(k.astype(F32).reshape(Bsz, S, H, dh), pos) * (dh ** -0.5)
    v = v.astype(F32).reshape(Bsz, S, H, dh)
    log_gamma = jnp.log1p(-jnp.exp2(-5.0 - jnp.arange(H, dtype=F32)))
    idx = jnp.arange(C, dtype=F32)
    rel = idx[:, None] - idx[None, :]
    inner_decay = jnp.where(rel >= 0, jnp.exp(log_gamma[:, None, None] * jnp.maximum(rel, 0.0)), 0.0)
    q_decay = jnp.exp(log_gamma[:, None] * (idx + 1.0))[None, :, :, None]
    k_decay = jnp.exp(log_gamma[:, None] * (C - 1.0 - idx))[None, :, :, None]
    chunk_decay = jnp.exp(log_gamma * C)[None, :, None, None]

    def to_chunks(t):
        return t.reshape(Bsz, N, C, H, dh).transpose(1, 0, 3, 2, 4)

    def step(state, qkv):
        qc, kc, vc = qkv
        scores = jnp.einsum('bhid,bhjd->bhij', qc, kc) * inner_decay
        o = (jnp.einsum('bhij,bhjd->bhid', scores, vc)
             + jnp.einsum('bhid,bhde->bhie', qc * q_decay, state))
        state = state * chunk_decay + jnp.einsum('bhjd,bhje->bhde', kc * k_decay, vc)
        return state, o

    state0 = jnp.zeros((Bsz, H, dh, dh), F32)
    _, o = lax.scan(step, state0, (to_chunks(q), to_chunks(k), to_chunks(v)))
    o = o.transpose(1, 0, 3, 2, 4).reshape(Bsz, S, H, dh)
    mu = jnp.mean(o, axis=-1, keepdims=True)
    var = jnp.mean(jnp.square(o - mu), axis=-1, keepdims=True)
    o = ((o - mu) * lax.rsqrt(var + GN_EPS)).reshape(Bsz, S, RET_WIDTH) * gn_w.astype(F32)
    return (jax.nn.silu(g.astype(F32)) * o).astype(g.dtype)


def token_mixers(h, pos, w_in, conv_w, ret_gn_w, w_out):
    proj = jnp.einsum('bsd,de->bse', h, w_in)
    cw, rw = CONV_WIDTH, RET_WIDTH
    cuts = [cw, 2 * cw, 3 * cw, 3 * cw + rw, 3 * cw + 2 * rw, 3 * cw + 3 * rw]
    cb, cc, cx, q, k, v, g = jnp.split(proj, cuts, axis=-1)
    y_conv = short_conv(cb, cc, cx, conv_w)
    y_ret = retention(q, k, v, g, pos, ret_gn_w)
    y = jnp.concatenate([y_conv, y_ret.astype(y_conv.dtype)], axis=-1)
    return jnp.einsum('bse,ed->bsd', y, w_out)


def routed_experts(hf, eidx, gw, w_gate, w_up, w_down):
    T, D = hf.shape
    TK = T * TOP_K
    n_blocks = -(-TK // MOE_BLOCK) + N_EXPERTS
    e_flat = eidx.reshape(TK).astype(jnp.int32)
    tok_flat = jnp.repeat(jnp.arange(T, dtype=jnp.int32), TOP_K)
    w_flat = gw.reshape(TK).astype(F32)
    e_s, tok_s, w_s = lax.sort((e_flat, tok_flat, w_flat), num_keys=1, is_stable=True)
    counts = jnp.bincount(e_flat, length=N_EXPERTS).astype(jnp.int32)
    start = jnp.cumsum(counts) - counts
    padded = (counts + MOE_BLOCK - 1) // MOE_BLOCK * MOE_BLOCK
    pad_end = jnp.cumsum(padded)
    pad_start = pad_end - padded
    dest = pad_start[e_s] + jnp.arange(TK, dtype=jnp.int32) - start[e_s]
    slot_tok = jnp.zeros((n_blocks * MOE_BLOCK,), jnp.int32).at[dest].set(tok_s)
    slot_w = jnp.zeros((n_blocks * MOE_BLOCK,), F32).at[dest].set(w_s)
    block_start = jnp.arange(n_blocks, dtype=jnp.int32) * MOE_BLOCK
    block_expert = jnp.minimum(jnp.searchsorted(pad_end, block_start, side='right'), N_EXPERTS - 1)
    block_used = block_start < pad_end[-1]

    def step(out, blk):
        tb, wb, eb, used = blk

        def compute(o):
            xb = hf[tb]
            y = swiglu(xb, w_gate[eb], w_up[eb], w_down[eb])
            return o.at[tb].add((y.astype(F32) * wb[:, None]).astype(o.dtype))

        return lax.cond(used, compute, lambda o: o, out), None

    out, _ = lax.scan(step, jnp.zeros_like(hf),
                      (slot_tok.reshape(n_blocks, MOE_BLOCK), slot_w.reshape(n_blocks, MOE_BLOCK),
                       block_expert, block_used))
    return out


def moe_ffn(h, w_router, router_bias, w_gate, w_up, w_down, ws_gate, ws_up, ws_down):
    Bsz, S, D = h.shape
    T = Bsz * S
    hf = h.reshape(T, D)
    shared = swiglu(hf, ws_gate, ws_up, ws_down)
    scores = jax.nn.sigmoid(jnp.einsum('td,ed->te', hf.astype(F32), w_router.astype(F32)))
    biased = scores + router_bias.astype(F32)
    grouped = biased.reshape(T, N_GROUPS, N_EXPERTS // N_GROUPS)
    group_score = jnp.sum(lax.top_k(grouped, 2)[0], axis=-1)
    _, top_groups = lax.top_k(group_score, TOPK_GROUPS)
    group_mask = jnp.sum(jax.nn.one_hot(top_groups, N_GROUPS, dtype=F32), axis=1) > 0
    expert_mask = jnp.repeat(group_mask, N_EXPERTS // N_GROUPS, axis=1)
    _, eidx = lax.top_k(jnp.where(expert_mask, biased, -jnp.inf), TOP_K)
    gw = jnp.take_along_axis(scores, eidx, axis=1)
    gw = gw / jnp.sum(gw, axis=-1, keepdims=True) * ROUTED_SCALE
    routed = routed_experts(hf, eidx, gw, w_gate, w_up, w_down)
    return (shared + routed.astype(shared.dtype)).reshape(Bsz, S, D)


def setup_inputs(seed: int = 0) -> dict:
    key = jax.random.key(seed)
    ks = jax.random.split(key, 24)
    L, D, E = DEPTH, D_MODEL, N_EXPERTS

    def nrm(k, shape, fan_in):
        return jax.random.normal(k, shape, F32) * (fan_in ** -0.5)

    def gain(k, shape):
        return 1.0 + 0.05 * jax.random.normal(k, shape, F32)

    x = jax.random.normal(ks[0], (BATCH, SEQ, D), F32)
    c = jax.random.normal(ks[1], (BATCH, D), F32)
    positions = (jax.random.randint(ks[2], (BATCH, 1), 0, 1024, dtype=jnp.int32)
                 + jnp.arange(SEQ, dtype=jnp.int32)[None, :])
    return {
        "x": x,
        "c": c,
        "positions": positions,
        "w_ada": 0.5 * nrm(ks[3], (L, D, 6 * D), D),
        "b_ada": 0.02 * jax.random.normal(ks[4], (L, 6 * D), F32),
        "g_mix_pre": gain(ks[5], (L, D)),
        "g_mix_post": gain(ks[6], (L, D)),
        "g_ffn_pre": gain(ks[7], (L, D)),
        "g_ffn_post": gain(ks[8], (L, D)),
        "w_in": nrm(ks[9], (L, D, IN_COLS), D),
        "conv_w": nrm(ks[10], (L, CONV_K, CONV_WIDTH), CONV_K),
        "ret_gn_w": gain(ks[11], (L, RET_WIDTH)),
        "w_out": nrm(ks[12], (L, MIX_WIDTH, D), MIX_WIDTH),
        "w_router": nrm(ks[13], (L, E, D), D),
        "router_bias": 0.01 * jax.random.normal(ks[14], (L, E), F32),
        "w_gate": nrm(ks[15], (L, E, D, EXPERT_FF), D),
        "w_up": nrm(ks[16], (L, E, D, EXPERT_FF), D),
        "w_down": nrm(ks[17], (L, E, EXPERT_FF, D), EXPERT_FF),
        "ws_gate": nrm(ks[18], (L, D, SHARED_FF), D),
        "ws_up": nrm(ks[19], (L, D, SHARED_FF), D),
        "ws_down": nrm(ks[20], (L, SHARED_FF, D), SHARED_FF),
    }


def reference(x, c, positions, w_ada, b_ada, g_mix_pre, g_mix_post, g_ffn_pre, g_ffn_post,
              w_in, conv_w, ret_gn_w, w_out, w_router, router_bias, w_gate, w_up, w_down,
              ws_gate, ws_up, ws_down):
    for l in range(DEPTH):
        mod = jnp.einsum('bd,de->be', jax.nn.silu(c), w_ada[l]) + b_ada[l]
        sh1, sc1, gt1, sh2, sc2, gt2 = jnp.split(mod[:, None, :], 6, axis=-1)
        h = rms_norm(x, g_mix_pre[l]) * (1.0 + sc1) + sh1
        m = token_mixers(h, positions, w_in[l], conv_w[l], ret_gn_w[l], w_out[l])
        x = x + gt1 * rms_norm(m, g_mix_post[l])
        h = rms_norm(x, g_ffn_pre[l]) * (1.0 + sc2) + sh2
        f = moe_ffn(h, w_router[l], router_bias[l], w_gate[l], w_up[l], w_down[l],
                    ws_gate[l], ws_up[l], ws_down[l])
        x = x + gt2 * rms_norm(f, g_ffn_post[l])
    return x
```

```python
import functools
import math

import jax
import jax.numpy as jnp
from jax import lax
from jax.experimental import pallas as pl
from jax.experimental.pallas import tpu as pltpu

F32 = jnp.float32
BF16 = jnp.bfloat16
U32 = jnp.uint32
I32 = jnp.int32

CONV_K = 3
RET_HEAD_DIM = 128
ROPE_THETA = 10000.0
TOP_K = 8
N_GROUPS = 8
TOPK_GROUPS = 4
ROUTED_SCALE = 2.5
NORM_EPS = 1e-6
GN_EPS = 1e-5

LANES = 128
SUBLANES = 8
VMEM_LIMIT_CAP = 56 * 1024 * 1024

RET_CHUNK = 256
MOE_BLK = 128
ROW_TILE = 128


def _cparams(n_axes, vmem_bytes):
    return pltpu.CompilerParams(
        dimension_semantics=("arbitrary",) * n_axes,
        vmem_limit_bytes=int(min(max(vmem_bytes, 16 * 1024 * 1024), VMEM_LIMIT_CAP)))


def _nbytes(shape, dtype):
    return math.prod(shape) * jnp.dtype(dtype).itemsize


def _dot(a, b):
    return jnp.dot(a, b, preferred_element_type=F32)


def _dot_nt(a, b):
    return lax.dot_general(a, b, (((1,), (1,)), ((), ())), preferred_element_type=F32)


def _dot_tn(a, b):
    return lax.dot_general(a, b, (((0,), (0,)), ((), ())), preferred_element_type=F32)


def _silu(v):
    return v * jax.nn.sigmoid(v)


def _pack_halves(lo, hi):
    lo_b = lax.bitcast_convert_type(lo.astype(BF16).astype(F32), U32)
    hi_b = lax.bitcast_convert_type(hi.astype(BF16).astype(F32), U32)
    return (lo_b >> 16) | (hi_b & jnp.uint32(0xFFFF0000))


def _unpack_halves(w):
    lo = lax.bitcast_convert_type(w << 16, F32)
    hi = lax.bitcast_convert_type(w & jnp.uint32(0xFFFF0000), F32)
    return lo, hi


def _ada_body(cb_ref, w_ref, b_ref, o_ref):
    s = _silu(cb_ref[...])
    for j in range(o_ref.shape[-1] // LANES):
        sl = slice(j * LANES, (j + 1) * LANES)
        o_ref[:, sl] = jnp.sum(w_ref[:, sl] * s, axis=0, keepdims=True) + b_ref[:, sl]


def _ada(c, w_ada, b_ada):
    D, N = w_ada.shape
    tn = min(1024, N)
    cb = jnp.broadcast_to(c.reshape(D, 1), (D, LANES))
    vmem = 2 * _nbytes((D, tn), F32) + 3 * _nbytes((D, LANES), F32) + (4 << 20)
    return pl.pallas_call(
        _ada_body,
        grid=(N // tn,),
        in_specs=[pl.BlockSpec((D, LANES), lambda j: (0, 0)),
                  pl.BlockSpec((D, tn), lambda j: (0, j)),
                  pl.BlockSpec((1, tn), lambda j: (0, j))],
        out_specs=pl.BlockSpec((1, tn), lambda j: (0, j)),
        out_shape=jax.ShapeDtypeStruct((1, N), F32),
        compiler_params=_cparams(1, vmem),
        name="ada",
    )(cb, w_ada, b_ada.reshape(1, N))


def _inproj_body(x_ref, g_ref, sc_ref, sh_ref, w_ref, o_ref, h_ref):
    @pl.when(pl.program_id(1) == 0)
    def _():
        xf = x_ref[...]
        r = xf * lax.rsqrt(jnp.mean(xf * xf, axis=-1, keepdims=True) + NORM_EPS)
        h_ref[...] = ((r * g_ref[...]) * (1.0 + sc_ref[...]) + sh_ref[...]).astype(BF16)

    o_ref[...] = _dot(h_ref[...], w_ref[...]).astype(o_ref.dtype)


def _inproj(x, g, sc, sh, w_bf, tn):
    T, D = x.shape
    N = w_bf.shape[1]
    tm = min(512, T)
    vec = pl.BlockSpec((1, D), lambda i, j: (0, 0))
    vmem = (2 * _nbytes((tm, D), F32) + _nbytes((tm, D), BF16) + 2 * _nbytes((D, tn), BF16)
            + 2 * _nbytes((tm, tn), BF16) + 2 * _nbytes((tm, tn), F32) + 2 * _nbytes((tm, D), F32) + (4 << 20))
    return pl.pallas_call(
        _inproj_body,
        grid=(T // tm, N // tn),
        in_specs=[pl.BlockSpec((tm, D), lambda i, j: (i, 0)), vec, vec, vec,
                  pl.BlockSpec((D, tn), lambda i, j: (0, j))],
        out_specs=pl.BlockSpec((tm, tn), lambda i, j: (i, j)),
        out_shape=jax.ShapeDtypeStruct((T, N), BF16),
        scratch_shapes=[pltpu.VMEM((tm, D), BF16)],
        compiler_params=_cparams(2, vmem),
        name="inproj",
    )(x, g, sc, sh, w_bf)


def _conv_body(cb_ref, cc_ref, cx_ref, w_ref, o_ref, u_ref):
    tc = cb_ref.shape[0]

    @pl.when(pl.program_id(1) == 0)
    def _():
        u_ref[0:SUBLANES, :] = jnp.zeros((SUBLANES, u_ref.shape[1]), F32)

    u = cc_ref[...].astype(F32) * cx_ref[...].astype(F32)
    u_ref[SUBLANES:SUBLANES + tc, :] = u
    u1 = u_ref[SUBLANES - 1:SUBLANES - 1 + tc, :]
    u2 = u_ref[SUBLANES - 2:SUBLANES - 2 + tc, :]
    conv = u2 * w_ref[0:1, :] + u1 * w_ref[1:2, :] + u * w_ref[2:3, :]
    o_ref[...] = (cb_ref[...].astype(F32) * conv).astype(o_ref.dtype)
    u_ref[0:SUBLANES, :] = u_ref[tc:tc + SUBLANES, :]


def _conv(proj, conv_w, CW):
    T = proj.shape[0]
    tc = min(512, T)
    tw = min(512, CW)
    nb = CW // tw
    vmem = 8 * _nbytes((tc, tw), BF16) + 8 * _nbytes((tc + SUBLANES, tw), F32) + (4 << 20)
    return pl.pallas_call(
        _conv_body,
        grid=(nb, T // tc),
        in_specs=[pl.BlockSpec((tc, tw), lambda j, i: (i, j)),
                  pl.BlockSpec((tc, tw), lambda j, i: (i, nb + j)),
                  pl.BlockSpec((tc, tw), lambda j, i: (i, 2 * nb + j)),
                  pl.BlockSpec((CONV_K, tw), lambda j, i: (0, j))],
        out_specs=pl.BlockSpec((tc, tw), lambda j, i: (i, j)),
        out_shape=jax.ShapeDtypeStruct((T, CW), BF16),
        scratch_shapes=[pltpu.VMEM((tc + SUBLANES, tw), F32)],
        compiler_params=_cparams(2, vmem),
        name="conv",
    )(proj, proj, proj, conv_w)


def _ret_body(q_ref, k_ref, v_ref, g_ref, pos_ref, inv_ref, gnw_ref, o_ref, state_ref, dec_ref, *, H, C):
    dh = RET_HEAD_DIM
    log_gamma = [math.log1p(-(2.0 ** (-5.0 - h))) for h in range(H)]

    @pl.when(pl.program_id(0) == 0)
    def _():
        state_ref[...] = jnp.zeros_like(state_ref)
        rel = (lax.broadcasted_iota(I32, (C, C), 0) - lax.broadcasted_iota(I32, (C, C), 1)).astype(F32)
        for h in range(H):
            dec_ref[h] = jnp.where(rel >= 0.0, jnp.exp(log_gamma[h] * jnp.maximum(rel, 0.0)), 0.0)

    ang = pos_ref[...] * inv_ref[...]
    cos2 = jnp.cos(ang)
    sin2 = jnp.where(lax.broadcasted_iota(I32, (C, dh), 1) < dh // 2, -jnp.sin(ang), jnp.sin(ang))
    ridx = lax.broadcasted_iota(I32, (C, 1), 0).astype(F32)

    def rope(t):
        return t * cos2 + pltpu.roll(t, dh // 2, 1) * sin2

    for h in range(H):
        sl = slice(h * dh, (h + 1) * dh)
        lg = log_gamma[h]
        qr = rope(q_ref[:, sl].astype(F32))
        kr = rope(k_ref[:, sl].astype(F32)) * (dh ** -0.5)
        v = v_ref[:, sl]
        q_decay = jnp.exp(lg * (ridx + 1.0))
        k_decay = jnp.exp(lg * (C - 1.0 - ridx))
        scores = _dot_nt(qr.astype(BF16), kr.astype(BF16)) * dec_ref[h]
        state = state_ref[h]
        o = _dot(scores.astype(BF16), v) + _dot((qr * q_decay).astype(BF16), state.astype(BF16))
        state_ref[h] = state * math.exp(lg * C) + _dot_tn((kr * k_decay).astype(BF16), v)
        mu = jnp.mean(o, axis=-1, keepdims=True)
        d = o - mu
        var = jnp.mean(d * d, axis=-1, keepdims=True)
        on = d * lax.rsqrt(var + GN_EPS) * gnw_ref[:, sl]
        o_ref[:, sl] = (_silu(g_ref[:, sl].astype(F32)) * on).astype(o_ref.dtype)


def _retention(proj, pos_f, gn_w, RW, col0_blocks):
    T = proj.shape[0]
    C = min(RET_CHUNK, T)
    H = RW // RET_HEAD_DIM
    half = RET_HEAD_DIM // 2
    inv = ROPE_THETA ** (-jnp.arange(half, dtype=F32) / half)
    inv2 = jnp.concatenate([inv, inv]).reshape(1, RET_HEAD_DIM)
    blk = lambda off: pl.BlockSpec((C, RW), lambda i: (i, col0_blocks + off))
    vmem = (10 * _nbytes((C, RW), BF16) + _nbytes((H, RET_HEAD_DIM, RET_HEAD_DIM), F32)
            + _nbytes((H, C, C), F32) + 24 * _nbytes((C, C), F32) + (8 << 20))
    return pl.pallas_call(
        functools.partial(_ret_body, H=H, C=C),
        grid=(T // C,),
        in_specs=[blk(0), blk(1), blk(2), blk(3),
                  pl.BlockSpec((C, 1), lambda i: (i, 0)),
                  pl.BlockSpec((1, RET_HEAD_DIM), lambda i: (0, 0)),
                  pl.BlockSpec((1, RW), lambda i: (0, 0))],
        out_specs=pl.BlockSpec((C, RW), lambda i: (i, 0)),
        out_shape=jax.ShapeDtypeStruct((T, RW), BF16),
        scratch_shapes=[pltpu.VMEM((H, RET_HEAD_DIM, RET_HEAD_DIM), F32), pltpu.VMEM((H, C, C), F32)],
        compiler_params=_cparams(1, vmem),
        name="ret",
    )(proj, proj, proj, proj, pos_f, inv2, gn_w)


def _outproj_body(yc_ref, yr_ref, w_ref, x_ref, gpost_ref, gt_ref, gpre_ref, sc_ref, sh_ref, wr_ref,
                  x1_ref, h2b_ref, h2p_ref, lt_ref, wrhi_ref, wrlo_ref):
    CW = yc_ref.shape[1]
    D = x_ref.shape[1]

    @pl.when(pl.program_id(0) == 0)
    def _():
        w = wr_ref[...]
        hi = w.astype(BF16)
        wrhi_ref[...] = hi
        wrlo_ref[...] = (w - hi.astype(F32)).astype(BF16)

    m = _dot(yc_ref[...], w_ref[0:CW, :]) + _dot(yr_ref[...], w_ref[CW:, :])
    mn = (m * lax.rsqrt(jnp.mean(m * m, axis=-1, keepdims=True) + NORM_EPS)) * gpost_ref[...]
    x1 = x_ref[...] + gt_ref[...] * mn
    x1_ref[...] = x1
    r = x1 * lax.rsqrt(jnp.mean(x1 * x1, axis=-1, keepdims=True) + NORM_EPS)
    h2 = (r * gpre_ref[...]) * (1.0 + sc_ref[...]) + sh_ref[...]
    hb = h2.astype(BF16)
    h2b_ref[...] = hb
    h2p_ref[...] = _pack_halves(h2[:, :D // 2], h2[:, D // 2:])
    hl = (h2 - hb.astype(F32)).astype(BF16)
    lt_ref[...] = (_dot_nt(wrhi_ref[...], hb) + _dot_nt(wrhi_ref[...], hl)) + _dot_nt(wrlo_ref[...], hb)


def _outproj(yc, yr, w_out_bf, x, gpost, gt, gpre, sc, sh, w_router):
    T, D = x.shape
    CW, RW = yc.shape[1], yr.shape[1]
    E = w_router.shape[0]
    tm = min(256, T)
    vec = pl.BlockSpec((1, D), lambda i: (0, 0))
    row = lambda w: pl.BlockSpec((tm, w), lambda i: (i, 0))
    vmem = (2 * _nbytes((CW + RW, D), BF16) + 2 * _nbytes((E, D), F32) + 2 * _nbytes((E, D), BF16)
            + 4 * _nbytes((tm, D), F32) + 4 * _nbytes((tm, D), BF16) + 2 * _nbytes((tm, D // 2), U32)
            + 2 * _nbytes((E, tm), F32) + 8 * _nbytes((tm, D), F32) + (4 << 20))
    return pl.pallas_call(
        _outproj_body,
        grid=(T // tm,),
        in_specs=[row(CW), row(RW), pl.BlockSpec((CW + RW, D), lambda i: (0, 0)), row(D),
                  vec, vec, vec, vec, vec, pl.BlockSpec((E, D), lambda i: (0, 0))],
        out_specs=[row(D), row(D), row(D // 2), pl.BlockSpec((E, tm), lambda i: (0, i))],
        out_shape=[jax.ShapeDtypeStruct((T, D), F32), jax.ShapeDtypeStruct((T, D), BF16),
                   jax.ShapeDtypeStruct((T, D // 2), U32), jax.ShapeDtypeStruct((E, T), F32)],
        scratch_shapes=[pltpu.VMEM((E, D), BF16), pltpu.VMEM((E, D), BF16)],
        compiler_params=_cparams(1, vmem),
        name="outproj",
    )(yc, yr, w_out_bf, x, gpost, gt, gpre, sc, sh, w_router)


def _route_body(lt_ref, bias_ref, eidx_ref, gw_ref, rank_ref, cnt_ref, carry_ref):
    E, TL = lt_ref.shape
    G, EG = N_GROUPS, E // N_GROUPS
    NEG, BIG = -jnp.inf, 1e9

    @pl.when(pl.program_id(0) == 0)
    def _():
        carry_ref[...] = jnp.zeros_like(carry_ref)

    scores = jax.nn.sigmoid(lt_ref[...]).reshape(G, EG, TL)
    biased = scores + bias_ref[...].reshape(G, EG, 1)
    eio = lax.broadcasted_iota(I32, (G, EG, TL), 1).astype(F32)
    eid = lax.broadcasted_iota(I32, (G, EG, TL), 0).astype(F32) * EG + eio
    gio = lax.broadcasted_iota(I32, (G, 1, TL), 0).astype(F32)

    m1 = jnp.max(biased, axis=1, keepdims=True)
    i1 = jnp.min(jnp.where(biased == m1, eio, BIG), axis=1, keepdims=True)
    m2 = jnp.max(jnp.where(eio == i1, NEG, biased), axis=1, keepdims=True)
    cur = m1 + m2
    gmask = jnp.zeros((G, 1, TL), F32)
    for _ in range(TOPK_GROUPS):
        mm = jnp.max(cur, axis=0, keepdims=True)
        ii = jnp.min(jnp.where(cur == mm, gio, BIG), axis=0, keepdims=True)
        sel = gio == ii
        gmask = jnp.where(sel, 1.0, gmask)
        cur = jnp.where(sel, NEG, cur)

    cur = jnp.where(gmask > 0.0, biased, NEG)
    sels, idxs, ws = [], [], []
    for _ in range(TOP_K):
        mm = jnp.max(jnp.max(cur, axis=1, keepdims=True), axis=0, keepdims=True)
        cand = jnp.where(cur == mm, eid, BIG)
        ii = jnp.min(jnp.min(cand, axis=1, keepdims=True), axis=0, keepdims=True)
        sel = eid == ii
        w = jnp.where(sel, scores, 0.0)
        ws.append(jnp.sum(jnp.sum(w, axis=1, keepdims=True), axis=0, keepdims=True).reshape(1, TL))
        idxs.append(ii.reshape(1, TL))
        sels.append(sel)
        cur = jnp.where(sel, NEG, cur)

    wsum = ws[0]
    for w in ws[1:]:
        wsum = wsum + w
    for k in range(TOP_K):
        eidx_ref[k:k + 1, :] = idxs[k].astype(I32)
        gw_ref[k:k + 1, :] = ws[k] / wsum * ROUTED_SCALE

    chosen = jnp.zeros((G, EG, TL), F32)
    for sel in sels:
        chosen = jnp.where(sel, 1.0, chosen)
    chosen2 = chosen.reshape(E, TL).astype(BF16)
    before = (lax.broadcasted_iota(I32, (TL, TL), 0) < lax.broadcasted_iota(I32, (TL, TL), 1))
    prefix = _dot(chosen2, jnp.where(before, 1.0, 0.0).astype(BF16))
    total = _dot(chosen2, jnp.ones((TL, TL), BF16))
    carry = carry_ref[...]
    rank_full = (carry + prefix).reshape(G, EG, TL)
    for k in range(TOP_K):
        r = jnp.where(sels[k], rank_full, 0.0)
        r = jnp.sum(jnp.sum(r, axis=1, keepdims=True), axis=0, keepdims=True).reshape(1, TL)
        rank_ref[k:k + 1, :] = r.astype(I32)
    carry = carry + total
    carry_ref[...] = carry
    cnt_ref[...] = carry[:, 0:LANES]


def _route(logits_t, bias):
    E, T = logits_t.shape
    TL = min(256, T)
    tok = pl.BlockSpec((TOP_K, TL), lambda i: (0, i))
    vmem = 64 * _nbytes((E, TL), F32) + (8 << 20)
    return pl.pallas_call(
        _route_body,
        grid=(T // TL,),
        in_specs=[pl.BlockSpec((E, TL), lambda i: (0, i)), pl.BlockSpec((E, 1), lambda i: (0, 0))],
        out_specs=[tok, tok, tok, pl.BlockSpec((E, LANES), lambda i: (0, 0))],
        out_shape=[jax.ShapeDtypeStruct((TOP_K, T), I32), jax.ShapeDtypeStruct((TOP_K, T), F32),
                   jax.ShapeDtypeStruct((TOP_K, T), I32), jax.ShapeDtypeStruct((E, LANES), F32)],
        scratch_shapes=[pltpu.VMEM((E, TL), F32)],
        compiler_params=_cparams(1, vmem),
        name="route",
    )(logits_t, bias.reshape(E, 1))


def _row_copy(src_hbm, s, dst, d, sem):
    return pltpu.make_async_copy(src_hbm.at[pl.ds(s, 1)], dst.at[pl.ds(d, 1)], sem)


def _dispatch_body(dest_ref, h_hbm, xs_hbm, sem):
    t0 = pl.program_id(0) * ROW_TILE

    def issue(t, c):
        for k in range(TOP_K):
            _row_copy(h_hbm, t0 + t, xs_hbm, dest_ref[0, 0, t * TOP_K + k], sem).start()
        return c

    def drain(t, c):
        for k in range(TOP_K):
            _row_copy(h_hbm, 0, xs_hbm, 0, sem).wait()
        return c

    lax.fori_loop(0, ROW_TILE, issue, 0)
    lax.fori_loop(0, ROW_TILE, drain, 0)


def _dispatch(dest_tiles, h2p):
    T, W = h2p.shape
    n_tiles = T // ROW_TILE
    return pl.pallas_call(
        _dispatch_body,
        grid=(n_tiles,),
        in_specs=[pl.BlockSpec((1, 1, ROW_TILE * TOP_K), lambda i: (i, 0, 0), memory_space=pltpu.SMEM),
                  pl.BlockSpec(memory_space=pl.ANY)],
        out_specs=pl.BlockSpec(memory_space=pl.ANY),
        out_shape=jax.ShapeDtypeStruct((T * TOP_K, W), U32),
        scratch_shapes=[pltpu.SemaphoreType.DMA(())],
        compiler_params=_cparams(1, 16 << 20),
        name="dispatch",
    )(dest_tiles, h2p)


def _gmm_body(tile_ref, grp_ref, lo_ref, hi_ref, tfirst_ref, gfirst_ref,
              xs_ref, wg_ref, wu_ref, wd_ref, ys_ref, wgb_ref, wub_ref, wdb_ref):
    v = pl.program_id(0)
    half = xs_ref.shape[1]

    @pl.when(gfirst_ref[v] == 1)
    def _():
        wgb_ref[...] = wg_ref[0].astype(BF16)
        wub_ref[...] = wu_ref[0].astype(BF16)
        wdb_ref[...] = wd_ref[0].astype(BF16)

    lo, hi = lo_ref[v], hi_ref[v]

    @pl.when(hi > lo)
    def _():
        xl, xh = _unpack_halves(xs_ref[...])
        xl, xh = xl.astype(BF16), xh.astype(BF16)
        gate = _dot(xl, wgb_ref[0:half, :]) + _dot(xh, wgb_ref[half:, :])
        up = _dot(xl, wub_ref[0:half, :]) + _dot(xh, wub_ref[half:, :])
        act = (_silu(gate) * up).astype(BF16)
        y = _dot(act, wdb_ref[...])
        packed = _pack_halves(y[:, :half], y[:, half:])
        rows = lax.broadcasted_iota(I32, packed.shape, 0)
        mine = (rows >= lo) & (rows < hi)

        @pl.when(tfirst_ref[v] == 1)
        def _():
            ys_ref[...] = jnp.where(mine, packed, jnp.uint32(0))

        @pl.when(tfirst_ref[v] == 0)
        def _():
            ys_ref[...] = jnp.where(mine, packed, ys_ref[...])


def _gmm(meta, xs, w_gate, w_up, w_down):
    R, half = xs.shape
    E, D, FF = w_gate.shape
    NV = meta[0].shape[0]
    rows = pl.BlockSpec((MOE_BLK, half), lambda v, tile, *_: (tile[v], 0))
    wspec = lambda a, b: pl.BlockSpec((1, a, b), lambda v, tile, grp, *_: (grp[v], 0, 0))
    vmem = (6 * _nbytes((D, FF), F32) + 3 * _nbytes((D, FF), BF16) + 4 * _nbytes((MOE_BLK, half), U32)
            + 6 * _nbytes((MOE_BLK, D), F32) + (6 << 20))
    return pl.pallas_call(
        _gmm_body,
        grid_spec=pltpu.PrefetchScalarGridSpec(
            num_scalar_prefetch=6,
            grid=(NV,),
            in_specs=[rows, wspec(D, FF), wspec(D, FF), wspec(FF, D)],
            out_specs=rows,
            scratch_shapes=[pltpu.VMEM((D, FF), BF16), pltpu.VMEM((D, FF), BF16), pltpu.VMEM((FF, D), BF16)]),
        out_shape=jax.ShapeDtypeStruct((R, half), U32),
        compiler_params=_cparams(1, vmem),
        name="gmm",
    )(*meta, xs, w_gate, w_up, w_down)


def _gmm_meta(counts, n_rows):
    E = counts.shape[0]
    n_tiles = n_rows // MOE_BLK
    NV = n_tiles + E - 1
    ends = jnp.cumsum(counts)
    starts = ends - counts
    first_tile = starts // MOE_BLK
    last_tile = jnp.maximum(ends - 1, 0) // MOE_BLK
    n_vis = jnp.where(counts > 0, last_tile - first_tile + 1, 0)
    v_end = jnp.cumsum(n_vis)
    v_start = v_end - n_vis
    total = v_end[-1]
    v = jnp.arange(NV, dtype=I32)
    vc = jnp.minimum(v, total - 1)
    grp = jnp.searchsorted(v_end, vc, side="right").astype(I32)
    tile = (first_tile[grp] + (vc - v_start[grp])).astype(I32)
    valid = v < total
    lo = jnp.where(valid, jnp.maximum(starts[grp], tile * MOE_BLK) - tile * MOE_BLK, 0).astype(I32)
    hi = jnp.where(valid, jnp.minimum(ends[grp], (tile + 1) * MOE_BLK) - tile * MOE_BLK, 0).astype(I32)
    prev = lambda a: jnp.concatenate([jnp.full((1,), -1, I32), a[:-1]])
    tfirst = ((tile != prev(tile)) & valid).astype(I32)
    gfirst = ((grp != prev(grp)) & valid).astype(I32)
    return (tile, grp, lo, hi, tfirst, gfirst), starts


def _shared_body(h_ref, wg_ref, wu_ref, wd_ref, o_ref):
    h = h_ref[...]
    act = (_silu(_dot(h, wg_ref[...])) * _dot(h, wu_ref[...])).astype(BF16)
    o_ref[...] = _dot(act, wd_ref[...]).astype(o_ref.dtype)


def _shared(h2b, wg_bf, wu_bf, wd_bf):
    T, D = h2b.shape
    SF = wg_bf.shape[1]
    tm = min(512, T)
    full = lambda a, b: pl.BlockSpec((a, b), lambda i: (0, 0))
    vmem = 6 * _nbytes((D, SF), BF16) + 4 * _nbytes((tm, D), BF16) + 3 * _nbytes((tm, D), F32) + (4 << 20)
    return pl.pallas_call(
        _shared_body,
        grid=(T // tm,),
        in_specs=[pl.BlockSpec((tm, D), lambda i: (i, 0)), full(D, SF), full(D, SF), full(SF, D)],
        out_specs=pl.BlockSpec((tm, D), lambda i: (i, 0)),
        out_shape=jax.ShapeDtypeStruct((T, D), BF16),
        compiler_params=_cparams(1, vmem),
        name="shared",
    )(h2b, wg_bf, wu_bf, wd_bf)


def _combine_body(dest_ref, ys_hbm, gw_ref, sh_ref, x1_ref, gt_ref, gpost_ref, o_ref, ybuf_ref, sem):
    half = ybuf_ref.shape[2]
    D = 2 * half

    def issue(t, c):
        for k in range(TOP_K):
            _row_copy(ys_hbm, dest_ref[0, 0, t * TOP_K + k], ybuf_ref.at[k], t, sem).start()
        return c

    def drain(t, c):
        for k in range(TOP_K):
            _row_copy(ys_hbm, 0, ybuf_ref.at[0], 0, sem).wait()
        return c

    lax.fori_loop(0, ROW_TILE, issue, 0)
    lax.fori_loop(0, ROW_TILE, drain, 0)

    rl = jnp.zeros((ROW_TILE, half), F32)
    rh = jnp.zeros((ROW_TILE, half), F32)
    for k in range(TOP_K):
        yl, yh = _unpack_halves(ybuf_ref[k])
        w = gw_ref[:, k:k + 1]
        rl = rl + yl * w
        rh = rh + yh * w
    fl = sh_ref[:, :half].astype(F32) + rl
    fh = sh_ref[:, half:].astype(F32) + rh
    ms = (jnp.sum(fl * fl, axis=-1, keepdims=True) + jnp.sum(fh * fh, axis=-1, keepdims=True)) / D
    r = lax.rsqrt(ms + NORM_EPS)
    o_ref[:, :half] = x1_ref[:, :half] + gt_ref[:, :half] * ((fl * r) * gpost_ref[:, :half])
    o_ref[:, half:] = x1_ref[:, half:] + gt_ref[:, half:] * ((fh * r) * gpost_ref[:, half:])


def _combine(dest_tiles, ys, gw, shared, x1, gt, gpost):
    T, D = x1.shape
    half = D // 2
    row = lambda w: pl.BlockSpec((ROW_TILE, w), lambda i: (i, 0))
    vec = pl.BlockSpec((1, D), lambda i: (0, 0))
    vmem = (_nbytes((TOP_K, ROW_TILE, half), U32) + 4 * _nbytes((ROW_TILE, D), F32) + 2 * _nbytes((ROW_TILE, D), BF16)
            + 8 * _nbytes((ROW_TILE, D), F32) + (4 << 20))
    return pl.pallas_call(
        _combine_body,
        grid=(T // ROW_TILE,),
        in_specs=[pl.BlockSpec((1, 1, ROW_TILE * TOP_K), lambda i: (i, 0, 0), memory_space=pltpu.SMEM),
                  pl.BlockSpec(memory_space=pl.ANY), row(TOP_K), row(D), row(D), vec, vec],
        out_specs=row(D),
        out_shape=jax.ShapeDtypeStruct((T, D), F32),
        scratch_shapes=[pltpu.VMEM((TOP_K, ROW_TILE, half), U32), pltpu.SemaphoreType.DMA(())],
        compiler_params=_cparams(1, vmem),
        name="combine",
    )(dest_tiles, ys, gw, shared, x1, gt, gpost)


def _layer(x, c, pos_f, w_ada, b_ada, g_mix_pre, g_mix_post, g_ffn_pre, g_ffn_post, w_in, conv_w, ret_gn_w,
           w_out, w_router, router_bias, w_gate, w_up, w_down, ws_gate, ws_up, ws_down):
    T, D = x.shape
    CW = conv_w.shape[-1]
    RW = ret_gn_w.shape[-1]
    assert CW == RW and w_in.shape[1] == 3 * CW + 4 * RW and w_out.shape[0] == CW + RW
    assert T % ROW_TILE == 0 and (T * TOP_K) % MOE_BLK == 0 and D % (2 * LANES) == 0
    vec = lambda a: a.reshape(1, D)

    mod = _ada(c, w_ada, b_ada)
    sh1, sc1, gt1, sh2, sc2, gt2 = (mod[:, i * D:(i + 1) * D] for i in range(6))

    proj = _inproj(x, vec(g_mix_pre), sc1, sh1, w_in.astype(BF16), tn=CW)
    y_conv = _conv(proj, conv_w, CW)
    y_ret = _retention(proj, pos_f, ret_gn_w.reshape(1, RW), RW, col0_blocks=3)
    x1, h2b, h2p, logits_t = _outproj(y_conv, y_ret, w_out.astype(BF16), x, vec(g_mix_post), gt1,
                                      vec(g_ffn_pre), sc2, sh2, w_router)

    eidx_t, gw_t, rank_t, cnt = _route(logits_t, router_bias)
    counts = cnt[:, 0].astype(I32)
    meta, starts = _gmm_meta(counts, T * TOP_K)
    dest = (starts[eidx_t] + rank_t).T
    dest_tiles = dest.reshape(T // ROW_TILE, 1, ROW_TILE * TOP_K)

    xs = _dispatch(dest_tiles, h2p)
    ys = _gmm(meta, xs, w_gate, w_up, w_down)
    shared = _shared(h2b, ws_gate.astype(BF16), ws_up.astype(BF16), ws_down.astype(BF16))
    return _combine(dest_tiles, ys, gw_t.T, shared, x1, gt2, vec(g_ffn_post))


def kernel(x, c, positions, w_ada, b_ada, g_mix_pre, g_mix_post, g_ffn_pre, g_ffn_post, w_in, conv_w, ret_gn_w,
           w_out, w_router, router_bias, w_gate, w_up, w_down, ws_gate, ws_up, ws_down):
    B, S, D = x.shape
    assert B == 1, "one sequence per call"
    xt = x.reshape(S, D)
    pos_f = positions.astype(F32).reshape(S, 1)
    for l in range(w_ada.shape[0]):
        xt = _layer(xt, c, pos_f, w_ada[l], b_ada[l], g_mix_pre[l], g_mix_post[l], g_ffn_pre[l], g_ffn_post[l],
                    w_in[l], conv_w[l], ret_gn_w[l], w_out[l], w_router[l], router_bias[l], w_gate[l], w_up[l],
                    w_down[l], ws_gate[l], ws_up[l], ws_down[l])
    return xt.reshape(B, S, D)
```

```python
import functools
import math

import jax
import jax.numpy as jnp
from jax import lax
from jax.experimental import pallas as pl
from jax.experimental.pallas import tpu as pltpu

F32 = jnp.float32
BF16 = jnp.bfloat16
U32 = jnp.uint32
I32 = jnp.int32

CONV_K = 3
RET_HEAD_DIM = 128
ROPE_THETA = 10000.0
TOP_K = 8
N_GROUPS = 8
TOPK_GROUPS = 4
ROUTED_SCALE = 2.5
NORM_EPS = 1e-6
GN_EPS = 1e-5

LANES = 128
SUBLANES = 8
VMEM_LIMIT_CAP = 56 * 1024 * 1024

RET_CHUNK = 256
MOE_BLK = 128
ROW_TILE = 128
OUTPROJ_SUB = 256


def _cparams(n_axes, vmem_bytes):
    return pltpu.CompilerParams(
        dimension_semantics=("arbitrary",) * n_axes,
        vmem_limit_bytes=int(min(max(vmem_bytes, 16 * 1024 * 1024), VMEM_LIMIT_CAP)))


def _nbytes(shape, dtype):
    return math.prod(shape) * jnp.dtype(dtype).itemsize


def _dot(a, b):
    return jnp.dot(a, b, preferred_element_type=F32)


def _dot_nt(a, b):
    return lax.dot_general(a, b, (((1,), (1,)), ((), ())), preferred_element_type=F32)


def _dot_tn(a, b):
    return lax.dot_general(a, b, (((0,), (0,)), ((), ())), preferred_element_type=F32)


def _silu(v):
    return v * jax.nn.sigmoid(v)


def _pack_halves(lo, hi):
    lo_b = lax.bitcast_convert_type(lo.astype(BF16).astype(F32), U32)
    hi_b = lax.bitcast_convert_type(hi.astype(BF16).astype(F32), U32)
    return (lo_b >> 16) | (hi_b & jnp.uint32(0xFFFF0000))


def _unpack_halves(w):
    lo = lax.bitcast_convert_type(w << 16, F32)
    hi = lax.bitcast_convert_type(w & jnp.uint32(0xFFFF0000), F32)
    return lo, hi


def _store_row_tiles(ref, val):
    rows, S = val.shape[0], val.shape[1] // LANES
    for s in range(S):
        ref[pl.ds(s, rows, stride=S), :] = val[:, s * LANES:(s + 1) * LANES]


def _load_row_chunk(ref, s, S):
    return ref[pl.ds(s, ref.shape[0] // S, stride=S), :]


def _load_row_tiles(ref, S):
    return jnp.concatenate([_load_row_chunk(ref, s, S) for s in range(S)], axis=1)


def _ada_body(cb_ref, w_ref, b_ref, o_ref):
    s = _silu(cb_ref[...])
    for j in range(o_ref.shape[-1] // LANES):
        sl = slice(j * LANES, (j + 1) * LANES)
        o_ref[:, sl] = jnp.sum(w_ref[:, sl] * s, axis=0, keepdims=True) + b_ref[:, sl]


def _ada(c, w_ada, b_ada):
    D, N = w_ada.shape
    tn = min(1024, N)
    cb = jnp.broadcast_to(c.reshape(D, 1), (D, LANES))
    vmem = 2 * _nbytes((D, tn), F32) + 3 * _nbytes((D, LANES), F32) + (4 << 20)
    return pl.pallas_call(
        _ada_body,
        grid=(N // tn,),
        in_specs=[pl.BlockSpec((D, LANES), lambda j: (0, 0)),
                  pl.BlockSpec((D, tn), lambda j: (0, j)),
                  pl.BlockSpec((1, tn), lambda j: (0, j))],
        out_specs=pl.BlockSpec((1, tn), lambda j: (0, j)),
        out_shape=jax.ShapeDtypeStruct((1, N), F32),
        compiler_params=_cparams(1, vmem),
        name="ada",
    )(cb, w_ada, b_ada.reshape(1, N))


def _inproj_body(x_ref, g_ref, sc_ref, sh_ref, w_ref, o_ref, h_ref):
    @pl.when(pl.program_id(1) == 0)
    def _():
        xf = x_ref[...]
        r = xf * lax.rsqrt(jnp.mean(xf * xf, axis=-1, keepdims=True) + NORM_EPS)
        h_ref[...] = ((r * g_ref[...]) * (1.0 + sc_ref[...]) + sh_ref[...]).astype(BF16)

    o_ref[...] = _dot(h_ref[...], w_ref[...]).astype(o_ref.dtype)


def _inproj(x, g, sc, sh, w_bf, tn):
    T, D = x.shape
    N = w_bf.shape[1]
    tm = min(512, T)
    vec = pl.BlockSpec((1, D), lambda i, j: (0, 0))
    vmem = (2 * _nbytes((tm, D), F32) + _nbytes((tm, D), BF16) + 2 * _nbytes((D, tn), BF16)
            + 2 * _nbytes((tm, tn), BF16) + 2 * _nbytes((tm, tn), F32) + 2 * _nbytes((tm, D), F32) + (4 << 20))
    return pl.pallas_call(
        _inproj_body,
        grid=(T // tm, N // tn),
        in_specs=[pl.BlockSpec((tm, D), lambda i, j: (i, 0)), vec, vec, vec,
                  pl.BlockSpec((D, tn), lambda i, j: (0, j))],
        out_specs=pl.BlockSpec((tm, tn), lambda i, j: (i, j)),
        out_shape=jax.ShapeDtypeStruct((T, N), BF16),
        scratch_shapes=[pltpu.VMEM((tm, D), BF16)],
        compiler_params=_cparams(2, vmem),
        name="inproj",
    )(x, g, sc, sh, w_bf)


def _conv_body(cb_ref, cc_ref, cx_ref, w_ref, o_ref, u_ref):
    tc = cb_ref.shape[0]

    @pl.when(pl.program_id(1) == 0)
    def _():
        u_ref[0:SUBLANES, :] = jnp.zeros((SUBLANES, u_ref.shape[1]), F32)

    u = cc_ref[...].astype(F32) * cx_ref[...].astype(F32)
    u_ref[SUBLANES:SUBLANES + tc, :] = u
    u1 = u_ref[SUBLANES - 1:SUBLANES - 1 + tc, :]
    u2 = u_ref[SUBLANES - 2:SUBLANES - 2 + tc, :]
    conv = u2 * w_ref[0:1, :] + u1 * w_ref[1:2, :] + u * w_ref[2:3, :]
    o_ref[...] = (cb_ref[...].astype(F32) * conv).astype(o_ref.dtype)
    u_ref[0:SUBLANES, :] = u_ref[tc:tc + SUBLANES, :]


def _conv(proj, conv_w, CW):
    T = proj.shape[0]
    tc = min(512, T)
    tw = min(512, CW)
    nb = CW // tw
    vmem = 8 * _nbytes((tc, tw), BF16) + 8 * _nbytes((tc + SUBLANES, tw), F32) + (4 << 20)
    return pl.pallas_call(
        _conv_body,
        grid=(nb, T // tc),
        in_specs=[pl.BlockSpec((tc, tw), lambda j, i: (i, j)),
                  pl.BlockSpec((tc, tw), lambda j, i: (i, nb + j)),
                  pl.BlockSpec((tc, tw), lambda j, i: (i, 2 * nb + j)),
                  pl.BlockSpec((CONV_K, tw), lambda j, i: (0, j))],
        out_specs=pl.BlockSpec((tc, tw), lambda j, i: (i, j)),
        out_shape=jax.ShapeDtypeStruct((T, CW), BF16),
        scratch_shapes=[pltpu.VMEM((tc + SUBLANES, tw), F32)],
        compiler_params=_cparams(2, vmem),
        name="conv",
    )(proj, proj, proj, conv_w)


def _ret_body(q_ref, k_ref, v_ref, g_ref, pos_ref, inv_ref, gnw_ref, o_ref, state_ref, dec_ref, *, H, C):
    dh = RET_HEAD_DIM
    log_gamma = [math.log1p(-(2.0 ** (-5.0 - h))) for h in range(H)]

    @pl.when(pl.program_id(0) == 0)
    def _():
        state_ref[...] = jnp.zeros_like(state_ref)
        rel = (lax.broadcasted_iota(I32, (C, C), 0) - lax.broadcasted_iota(I32, (C, C), 1)).astype(F32)
        for h in range(H):
            dec_ref[h] = jnp.where(rel >= 0.0, jnp.exp(log_gamma[h] * jnp.maximum(rel, 0.0)), 0.0)

    ang = pos_ref[...] * inv_ref[...]
    cos2 = jnp.cos(ang)
    sin2 = jnp.where(lax.broadcasted_iota(I32, (C, dh), 1) < dh // 2, -jnp.sin(ang), jnp.sin(ang))
    ridx = lax.broadcasted_iota(I32, (C, 1), 0).astype(F32)

    def rope(t):
        return t * cos2 + pltpu.roll(t, dh // 2, 1) * sin2

    for h in range(H):
        sl = slice(h * dh, (h + 1) * dh)
        lg = log_gamma[h]
        qr = rope(q_ref[:, sl].astype(F32))
        kr = rope(k_ref[:, sl].astype(F32)) * (dh ** -0.5)
        v = v_ref[:, sl]
        q_decay = jnp.exp(lg * (ridx + 1.0))
        k_decay = jnp.exp(lg * (C - 1.0 - ridx))
        scores = _dot_nt(qr.astype(BF16), kr.astype(BF16)) * dec_ref[h]
        state = state_ref[h]
        o = _dot(scores.astype(BF16), v) + _dot((qr * q_decay).astype(BF16), state.astype(BF16))
        state_ref[h] = state * math.exp(lg * C) + _dot_tn((kr * k_decay).astype(BF16), v)
        mu = jnp.mean(o, axis=-1, keepdims=True)
        d = o - mu
        var = jnp.mean(d * d, axis=-1, keepdims=True)
        on = d * lax.rsqrt(var + GN_EPS) * gnw_ref[:, sl]
        o_ref[:, sl] = (_silu(g_ref[:, sl].astype(F32)) * on).astype(o_ref.dtype)


def _retention(proj, pos_f, gn_w, RW, col0_blocks):
    T = proj.shape[0]
    C = min(RET_CHUNK, T)
    H = RW // RET_HEAD_DIM
    half = RET_HEAD_DIM // 2
    inv = ROPE_THETA ** (-jnp.arange(half, dtype=F32) / half)
    inv2 = jnp.concatenate([inv, inv]).reshape(1, RET_HEAD_DIM)
    blk = lambda off: pl.BlockSpec((C, RW), lambda i: (i, col0_blocks + off))
    vmem = (10 * _nbytes((C, RW), BF16) + _nbytes((H, RET_HEAD_DIM, RET_HEAD_DIM), F32)
            + _nbytes((H, C, C), F32) + 24 * _nbytes((C, C), F32) + (8 << 20))
    return pl.pallas_call(
        functools.partial(_ret_body, H=H, C=C),
        grid=(T // C,),
        in_specs=[blk(0), blk(1), blk(2), blk(3),
                  pl.BlockSpec((C, 1), lambda i: (i, 0)),
                  pl.BlockSpec((1, RET_HEAD_DIM), lambda i: (0, 0)),
                  pl.BlockSpec((1, RW), lambda i: (0, 0))],
        out_specs=pl.BlockSpec((C, RW), lambda i: (i, 0)),
        out_shape=jax.ShapeDtypeStruct((T, RW), BF16),
        scratch_shapes=[pltpu.VMEM((H, RET_HEAD_DIM, RET_HEAD_DIM), F32), pltpu.VMEM((H, C, C), F32)],
        compiler_params=_cparams(1, vmem),
        name="ret",
    )(proj, proj, proj, proj, pos_f, inv2, gn_w)


def _outproj_body(yc_ref, yr_ref, w_ref, x_ref, gpost_ref, gt_ref, gpre_ref, sc_ref, sh_ref, wr_ref,
                  x1_ref, h2b_ref, h2p_ref, lt_ref, wrhi_ref, wrlo_ref):
    CW = yc_ref.shape[1]
    D = x_ref.shape[1]

    @pl.when(pl.program_id(0) == 0)
    def _():
        w = wr_ref[...]
        hi = w.astype(BF16)
        wrhi_ref[...] = hi
        wrlo_ref[...] = (w - hi.astype(F32)).astype(BF16)

    S = D // 2 // LANES
    tm = x_ref.shape[0]
    sub = min(tm, OUTPROJ_SUB)
    for r0 in range(0, tm, sub):
        rows = slice(r0, r0 + sub)
        m = _dot(yc_ref[rows, :], w_ref[0:CW, :]) + _dot(yr_ref[rows, :], w_ref[CW:, :])
        mn = (m * lax.rsqrt(jnp.mean(m * m, axis=-1, keepdims=True) + NORM_EPS)) * gpost_ref[...]
        x1 = x_ref[rows, :] + gt_ref[...] * mn
        x1_ref[rows, :] = x1
        r = x1 * lax.rsqrt(jnp.mean(x1 * x1, axis=-1, keepdims=True) + NORM_EPS)
        h2 = (r * gpre_ref[...]) * (1.0 + sc_ref[...]) + sh_ref[...]
        hb = h2.astype(BF16)
        h2b_ref[rows, :] = hb
        _store_row_tiles(h2p_ref.at[pl.ds(r0 * S, sub * S)], _pack_halves(h2[:, :D // 2], h2[:, D // 2:]))
        hl = (h2 - hb.astype(F32)).astype(BF16)
        lt_ref[:, rows] = (_dot_nt(wrhi_ref[...], hb) + _dot_nt(wrhi_ref[...], hl)) + _dot_nt(wrlo_ref[...], hb)


def _outproj(yc, yr, w_out_bf, x, gpost, gt, gpre, sc, sh, w_router):
    T, D = x.shape
    CW, RW = yc.shape[1], yr.shape[1]
    E = w_router.shape[0]
    S = D // 2 // LANES
    tm = min(256, T)
    vec = pl.BlockSpec((1, D), lambda i: (0, 0))
    row = lambda w: pl.BlockSpec((tm, w), lambda i: (i, 0))
    vmem = (2 * _nbytes((CW + RW, D), BF16) + 2 * _nbytes((E, D), F32) + 2 * _nbytes((E, D), BF16)
            + 4 * _nbytes((tm, D), F32) + 4 * _nbytes((tm, D), BF16) + 2 * _nbytes((tm, D // 2), U32)
            + 2 * _nbytes((E, tm), F32) + 8 * _nbytes((tm, D), F32) + (4 << 20))
    return pl.pallas_call(
        _outproj_body,
        grid=(T // tm,),
        in_specs=[row(CW), row(RW), pl.BlockSpec((CW + RW, D), lambda i: (0, 0)), row(D),
                  vec, vec, vec, vec, vec, pl.BlockSpec((E, D), lambda i: (0, 0))],
        out_specs=[row(D), row(D), pl.BlockSpec((tm * S, LANES), lambda i: (i, 0)),
                   pl.BlockSpec((E, tm), lambda i: (0, i))],
        out_shape=[jax.ShapeDtypeStruct((T, D), F32), jax.ShapeDtypeStruct((T, D), BF16),
                   jax.ShapeDtypeStruct((T * S, LANES), U32), jax.ShapeDtypeStruct((E, T), F32)],
        scratch_shapes=[pltpu.VMEM((E, D), BF16), pltpu.VMEM((E, D), BF16)],
        compiler_params=_cparams(1, vmem),
        name="outproj",
    )(yc, yr, w_out_bf, x, gpost, gt, gpre, sc, sh, w_router)


def _route_body(lt_ref, bias_ref, eidx_ref, gw_ref, rank_ref, cnt_ref, carry_ref):
    E, TL = lt_ref.shape
    G, EG = N_GROUPS, E // N_GROUPS
    NEG, BIG = -jnp.inf, 1e9

    @pl.when(pl.program_id(0) == 0)
    def _():
        carry_ref[...] = jnp.zeros_like(carry_ref)

    scores = jax.nn.sigmoid(lt_ref[...]).reshape(G, EG, TL)
    biased = scores + bias_ref[...].reshape(G, EG, 1)
    eio = lax.broadcasted_iota(I32, (G, EG, TL), 1).astype(F32)
    eid = lax.broadcasted_iota(I32, (G, EG, TL), 0).astype(F32) * EG + eio
    gio = lax.broadcasted_iota(I32, (G, 1, TL), 0).astype(F32)

    m1 = jnp.max(biased, axis=1, keepdims=True)
    i1 = jnp.min(jnp.where(biased == m1, eio, BIG), axis=1, keepdims=True)
    m2 = jnp.max(jnp.where(eio == i1, NEG, biased), axis=1, keepdims=True)
    cur = m1 + m2
    gmask = jnp.zeros((G, 1, TL), F32)
    for _ in range(TOPK_GROUPS):
        mm = jnp.max(cur, axis=0, keepdims=True)
        ii = jnp.min(jnp.where(cur == mm, gio, BIG), axis=0, keepdims=True)
        sel = gio == ii
        gmask = jnp.where(sel, 1.0, gmask)
        cur = jnp.where(sel, NEG, cur)

    cur = jnp.where(gmask > 0.0, biased, NEG)
    sels, idxs, ws = [], [], []
    for _ in range(TOP_K):
        mm = jnp.max(jnp.max(cur, axis=1, keepdims=True), axis=0, keepdims=True)
        cand = jnp.where(cur == mm, eid, BIG)
        ii = jnp.min(jnp.min(cand, axis=1, keepdims=True), axis=0, keepdims=True)
        sel = eid == ii
        w = jnp.where(sel, scores, 0.0)
        ws.append(jnp.sum(jnp.sum(w, axis=1, keepdims=True), axis=0, keepdims=True).reshape(1, TL))
        idxs.append(ii.reshape(1, TL))
        sels.append(sel)
        cur = jnp.where(sel, NEG, cur)

    wsum = ws[0]
    for w in ws[1:]:
        wsum = wsum + w
    for k in range(TOP_K):
        eidx_ref[k:k + 1, :] = idxs[k].astype(I32)
        gw_ref[k:k + 1, :] = ws[k] / wsum * ROUTED_SCALE

    chosen = jnp.zeros((G, EG, TL), F32)
    for sel in sels:
        chosen = jnp.where(sel, 1.0, chosen)
    chosen2 = chosen.reshape(E, TL).astype(BF16)
    before = (lax.broadcasted_iota(I32, (TL, TL), 0) < lax.broadcasted_iota(I32, (TL, TL), 1))
    prefix = _dot(chosen2, jnp.where(before, 1.0, 0.0).astype(BF16))
    total = _dot(chosen2, jnp.ones((TL, TL), BF16))
    carry = carry_ref[...]
    rank_full = (carry + prefix).reshape(G, EG, TL)
    for k in range(TOP_K):
        r = jnp.where(sels[k], rank_full, 0.0)
        r = jnp.sum(jnp.sum(r, axis=1, keepdims=True), axis=0, keepdims=True).reshape(1, TL)
        rank_ref[k:k + 1, :] = r.astype(I32)
    carry = carry + total
    carry_ref[...] = carry
    cnt_ref[...] = carry[:, 0:LANES]


def _route(logits_t, bias):
    E, T = logits_t.shape
    TL = min(256, T)
    tok = pl.BlockSpec((TOP_K, TL), lambda i: (0, i))
    vmem = 64 * _nbytes((E, TL), F32) + (8 << 20)
    return pl.pallas_call(
        _route_body,
        grid=(T // TL,),
        in_specs=[pl.BlockSpec((E, TL), lambda i: (0, i)), pl.BlockSpec((E, 1), lambda i: (0, 0))],
        out_specs=[tok, tok, tok, pl.BlockSpec((E, LANES), lambda i: (0, 0))],
        out_shape=[jax.ShapeDtypeStruct((TOP_K, T), I32), jax.ShapeDtypeStruct((TOP_K, T), F32),
                   jax.ShapeDtypeStruct((TOP_K, T), I32), jax.ShapeDtypeStruct((E, LANES), F32)],
        scratch_shapes=[pltpu.VMEM((E, TL), F32)],
        compiler_params=_cparams(1, vmem),
        name="route",
    )(logits_t, bias.reshape(E, 1))


def _dest_body(eidx_ref, rank_ref, starts_ref, dest_ref):
    E, TL = starts_ref.shape[0], eidx_ref.shape[1]
    eid = lax.broadcasted_iota(I32, (E, TL), 0)
    starts = starts_ref[...]
    for k in range(TOP_K):
        base = jnp.sum(jnp.where(eid == eidx_ref[k:k + 1, :], starts, 0.0), axis=0, keepdims=True)
        dest_ref[k:k + 1, :] = base.astype(I32) + rank_ref[k:k + 1, :]


def _dest(eidx_t, rank_t, starts):
    K, T = eidx_t.shape
    E = starts.shape[0]
    TL = min(512, T)
    tok = pl.BlockSpec((K, TL), lambda i: (0, i))
    return pl.pallas_call(
        _dest_body,
        grid=(T // TL,),
        in_specs=[tok, tok, pl.BlockSpec((E, 1), lambda i: (0, 0))],
        out_specs=tok,
        out_shape=jax.ShapeDtypeStruct((K, T), I32),
        compiler_params=_cparams(1, 16 * _nbytes((E, TL), F32) + (4 << 20)),
        name="dest",
    )(eidx_t, rank_t, starts.astype(F32).reshape(E, 1))


def _row_span(r, S):
    if isinstance(r, int):
        return pl.ds(r * S, S)
    return pl.ds(pl.multiple_of(r * S, S), S)


def _row_copy(src, s, dst, d, sem, S):
    return pltpu.make_async_copy(src.at[_row_span(s, S)], dst.at[_row_span(d, S)], sem)


def _dispatch_body(dest_ref, h_hbm, xs_hbm, sems, *, S):
    i = pl.program_id(0)
    t0 = i * ROW_TILE

    def drain(sem):
        for _ in range(ROW_TILE * TOP_K):
            _row_copy(h_hbm, 0, xs_hbm, 0, sem, S).wait()

    for t in range(ROW_TILE):
        for k in range(TOP_K):
            _row_copy(h_hbm, t0 + t, xs_hbm, dest_ref[0, 0, t * TOP_K + k], sems.at[i % 2], S).start()

    @pl.when(i > 0)
    def _():
        drain(sems.at[(i + 1) % 2])

    @pl.when(i == pl.num_programs(0) - 1)
    def _():
        drain(sems.at[i % 2])


def _dispatch(dest_tiles, h2p, S):
    T = h2p.shape[0] // S
    n_tiles = T // ROW_TILE
    return pl.pallas_call(
        functools.partial(_dispatch_body, S=S),
        grid=(n_tiles,),
        in_specs=[pl.BlockSpec((1, 1, ROW_TILE * TOP_K), lambda i: (i, 0, 0), memory_space=pltpu.SMEM),
                  pl.BlockSpec(memory_space=pl.ANY)],
        out_specs=pl.BlockSpec(memory_space=pl.ANY),
        out_shape=jax.ShapeDtypeStruct((T * TOP_K * S, LANES), U32),
        scratch_shapes=[pltpu.SemaphoreType.DMA((2,))],
        compiler_params=_cparams(1, 16 << 20),
        name="dispatch",
    )(dest_tiles, h2p)


def _gmm_body(tile_ref, grp_ref, lo_ref, hi_ref, tfirst_ref, gfirst_ref, gnext_ref,
              xs_ref, wg_hbm, wu_hbm, wd_hbm, ys_ref,
              wgf_ref, wuf_ref, wdf_ref, wgb_ref, wub_ref, wdb_ref, sems):
    v = pl.program_id(0)
    half = wgb_ref.shape[0] // 2
    S = half // LANES
    landing = ((wg_hbm, wgf_ref, wgb_ref), (wu_hbm, wuf_ref, wub_ref), (wd_hbm, wdf_ref, wdb_ref))

    def fetch(j, g):
        return pltpu.make_async_copy(landing[j][0].at[g], landing[j][1], sems.at[j])

    @pl.when(v == 0)
    def _():
        for j in range(3):
            fetch(j, grp_ref[0]).start()

    @pl.when(gfirst_ref[v] == 1)
    def _():
        nxt = gnext_ref[v]
        for j in range(3):
            fetch(j, grp_ref[v]).wait()
            landing[j][2][...] = landing[j][1][...].astype(BF16)

            @pl.when(nxt >= 0)
            def _():
                fetch(j, nxt).start()

    lo, hi = lo_ref[v], hi_ref[v]

    @pl.when(hi > lo)
    def _():
        xl, xh = _unpack_halves(_load_row_tiles(xs_ref, S))
        xl, xh = xl.astype(BF16), xh.astype(BF16)
        gate = _dot(xl, wgb_ref[0:half, :]) + _dot(xh, wgb_ref[half:, :])
        up = _dot(xl, wub_ref[0:half, :]) + _dot(xh, wub_ref[half:, :])
        act = (_silu(gate) * up).astype(BF16)
        y = _dot(act, wdb_ref[...])
        packed = _pack_halves(y[:, :half], y[:, half:])
        rows = lax.broadcasted_iota(I32, packed.shape, 0)
        mine = (rows >= lo) & (rows < hi)

        @pl.when(tfirst_ref[v] == 1)
        def _():
            _store_row_tiles(ys_ref, jnp.where(mine, packed, jnp.uint32(0)))

        @pl.when(tfirst_ref[v] == 0)
        def _():
            _store_row_tiles(ys_ref, jnp.where(mine, packed, _load_row_tiles(ys_ref, S)))


def _gmm(meta, xs, w_gate, w_up, w_down):
    E, D, FF = w_gate.shape
    half = D // 2
    S = half // LANES
    NV = meta[0].shape[0]
    n_meta = len(meta)
    rows = pl.BlockSpec((MOE_BLK * S, LANES), lambda v, tile, *_: (tile[v], 0))
    hbm = pl.BlockSpec(memory_space=pl.ANY)
    vmem = (3 * _nbytes((D, FF), F32) + 3 * _nbytes((D, FF), BF16) + 4 * _nbytes((MOE_BLK, half), U32)
            + 8 * _nbytes((MOE_BLK, D), F32) + (6 << 20))
    return pl.pallas_call(
        _gmm_body,
        grid_spec=pltpu.PrefetchScalarGridSpec(
            num_scalar_prefetch=n_meta,
            grid=(NV,),
            in_specs=[rows, hbm, hbm, hbm],
            out_specs=rows,
            scratch_shapes=[pltpu.VMEM((D, FF), F32), pltpu.VMEM((D, FF), F32), pltpu.VMEM((FF, D), F32),
                            pltpu.VMEM((D, FF), BF16), pltpu.VMEM((D, FF), BF16), pltpu.VMEM((FF, D), BF16),
                            pltpu.SemaphoreType.DMA((3,))]),
        out_shape=jax.ShapeDtypeStruct(xs.shape, U32),
        compiler_params=_cparams(1, vmem),
        name="gmm",
    )(*meta, xs, w_gate, w_up, w_down)


def _gmm_meta(counts, n_rows):
    E = counts.shape[0]
    n_tiles = n_rows // MOE_BLK
    NV = n_tiles + E - 1
    eids = jnp.arange(E, dtype=I32)
    ends = jnp.cumsum(counts)
    starts = ends - counts
    first_tile = starts // MOE_BLK
    last_tile = jnp.maximum(ends - 1, 0) // MOE_BLK
    n_vis = jnp.where(counts > 0, last_tile - first_tile + 1, 0)
    v_end = jnp.cumsum(n_vis)
    v_start = v_end - n_vis
    total = v_end[-1]
    later = jnp.where((eids[None, :] > eids[:, None]) & (counts[None, :] > 0), eids[None, :], E)
    nxt = jnp.min(later, axis=1)
    nxt = jnp.where(nxt == E, -1, nxt)
    v = jnp.arange(NV, dtype=I32)
    vc = jnp.minimum(v, total - 1)
    grp = jnp.sum((v_end[None, :] <= vc[:, None]).astype(I32), axis=1)
    onehot = grp[:, None] == eids[None, :]
    pick = lambda a: jnp.sum(jnp.where(onehot, a[None, :], 0), axis=1)
    tile = pick(first_tile) + (vc - pick(v_start))
    valid = v < total
    lo = jnp.where(valid, jnp.maximum(pick(starts), tile * MOE_BLK) - tile * MOE_BLK, 0)
    hi = jnp.where(valid, jnp.minimum(pick(ends), (tile + 1) * MOE_BLK) - tile * MOE_BLK, 0)
    prev = lambda a: jnp.concatenate([jnp.full((1,), -1, I32), a[:-1]])
    tfirst = (tile != prev(tile)) & valid
    gfirst = (grp != prev(grp)) & valid
    meta = (tile, grp, lo, hi, tfirst, gfirst, pick(nxt))
    return tuple(m.astype(I32) for m in meta), starts


def _shared_body(h_ref, wg_ref, wu_ref, wd_ref, o_ref):
    h = h_ref[...]
    act = (_silu(_dot(h, wg_ref[...])) * _dot(h, wu_ref[...])).astype(BF16)
    o_ref[...] = _dot(act, wd_ref[...]).astype(o_ref.dtype)


def _shared(h2b, wg_bf, wu_bf, wd_bf):
    T, D = h2b.shape
    SF = wg_bf.shape[1]
    tm = min(512, T)
    full = lambda a, b: pl.BlockSpec((a, b), lambda i: (0, 0))
    vmem = 6 * _nbytes((D, SF), BF16) + 4 * _nbytes((tm, D), BF16) + 3 * _nbytes((tm, D), F32) + (4 << 20)
    return pl.pallas_call(
        _shared_body,
        grid=(T // tm,),
        in_specs=[pl.BlockSpec((tm, D), lambda i: (i, 0)), full(D, SF), full(D, SF), full(SF, D)],
        out_specs=pl.BlockSpec((tm, D), lambda i: (i, 0)),
        out_shape=jax.ShapeDtypeStruct((T, D), BF16),
        compiler_params=_cparams(1, vmem),
        name="shared",
    )(h2b, wg_bf, wu_bf, wd_bf)


def _combine_body(dest0_ref, destn_ref, ys_hbm, gw_ref, sh_ref, x1_ref, gt_ref, gpost_ref, o_ref, ybuf_ref, sems):
    i = pl.program_id(0)
    last = pl.num_programs(0) - 1
    D = x1_ref.shape[1]
    half = D // 2
    S = half // LANES

    def gather(dest_ref, slot):
        for t in range(ROW_TILE):
            for k in range(TOP_K):
                _row_copy(ys_hbm, dest_ref[0, 0, t * TOP_K + k], ybuf_ref.at[slot, k], t,
                          sems.at[slot], S).start(priority=k % 2)

    def drain(slot):
        for _ in range(ROW_TILE * TOP_K):
            _row_copy(ys_hbm, 0, ybuf_ref.at[slot, 0], 0, sems.at[slot], S).wait()

    @pl.when(i == 0)
    def _():
        gather(dest0_ref, 0)

    gather(destn_ref, (i + 1) % 2)
    drain(i % 2)
    slot = i % 2

    wb = [jnp.broadcast_to(gw_ref[:, k:k + 1], (ROW_TILE, LANES)) for k in range(TOP_K)]
    ssq = jnp.zeros((ROW_TILE, 1), F32)
    for s in range(S):
        c_lo = slice(s * LANES, (s + 1) * LANES)
        c_hi = slice(half + s * LANES, half + (s + 1) * LANES)
        fl = sh_ref[:, c_lo].astype(F32)
        fh = sh_ref[:, c_hi].astype(F32)
        for k in range(TOP_K):
            yl, yh = _unpack_halves(_load_row_chunk(ybuf_ref.at[slot, k], s, S))
            fl = fl + yl * wb[k]
            fh = fh + yh * wb[k]
        ssq = ssq + jnp.sum(fl * fl, axis=-1, keepdims=True) + jnp.sum(fh * fh, axis=-1, keepdims=True)
        o_ref[:, c_lo] = fl
        o_ref[:, c_hi] = fh
    r = lax.rsqrt(ssq / D + NORM_EPS)
    o_ref[...] = x1_ref[...] + gt_ref[...] * ((o_ref[...] * r) * gpost_ref[...])

    @pl.when(i == last)
    def _():
        drain((i + 1) % 2)


def _combine(dest_tiles, ys, gw, shared, x1, gt, gpost):
    T, D = x1.shape
    half = D // 2
    n_tiles = T // ROW_TILE
    row = lambda w: pl.BlockSpec((ROW_TILE, w), lambda i: (i, 0))
    vec = pl.BlockSpec((1, D), lambda i: (0, 0))
    idx = lambda f: pl.BlockSpec((1, 1, ROW_TILE * TOP_K), f, memory_space=pltpu.SMEM)
    S = half // LANES
    vmem = (2 * _nbytes((TOP_K, ROW_TILE, half), U32) + 4 * _nbytes((ROW_TILE, D), F32)
            + 2 * _nbytes((ROW_TILE, D), BF16) + 8 * _nbytes((ROW_TILE, D), F32) + (4 << 20))
    return pl.pallas_call(
        _combine_body,
        grid=(n_tiles,),
        in_specs=[idx(lambda i: (0, 0, 0)), idx(lambda i: (jnp.minimum(i + 1, n_tiles - 1), 0, 0)),
                  pl.BlockSpec(memory_space=pl.ANY), row(TOP_K), row(D), row(D), vec, vec],
        out_specs=row(D),
        out_shape=jax.ShapeDtypeStruct((T, D), F32),
        scratch_shapes=[pltpu.VMEM((2, TOP_K, ROW_TILE * S, LANES), U32), pltpu.SemaphoreType.DMA((2,))],
        compiler_params=_cparams(1, vmem),
        name="combine",
    )(dest_tiles, dest_tiles, ys, gw, shared, x1, gt, gpost)


def _layer(x, c, pos_f, w_ada, b_ada, g_mix_pre, g_mix_post, g_ffn_pre, g_ffn_post, w_in, conv_w, ret_gn_w,
           w_out, w_router, router_bias, w_gate, w_up, w_down, ws_gate, ws_up, ws_down):
    T, D = x.shape
    CW = conv_w.shape[-1]
    RW = ret_gn_w.shape[-1]
    assert CW == RW and w_in.shape[1] == 3 * CW + 4 * RW and w_out.shape[0] == CW + RW
    assert T % ROW_TILE == 0 and (T * TOP_K) % MOE_BLK == 0 and D % (2 * LANES) == 0
    vec = lambda a: a.reshape(1, D)

    mod = _ada(c, w_ada, b_ada)
    sh1, sc1, gt1, sh2, sc2, gt2 = (mod[:, i * D:(i + 1) * D] for i in range(6))

    proj = _inproj(x, vec(g_mix_pre), sc1, sh1, w_in.astype(BF16), tn=CW)
    y_conv = _conv(proj, conv_w, CW)
    y_ret = _retention(proj, pos_f, ret_gn_w.reshape(1, RW), RW, col0_blocks=3)
    x1, h2b, h2p, logits_t = _outproj(y_conv, y_ret, w_out.astype(BF16), x, vec(g_mix_post), gt1,
                                      vec(g_ffn_pre), sc2, sh2, w_router)

    eidx_t, gw_t, rank_t, cnt = _route(logits_t, router_bias)
    counts = cnt[:, 0].astype(I32)
    meta, starts = _gmm_meta(counts, T * TOP_K)
    dest = _dest(eidx_t, rank_t, starts).T
    dest_tiles = dest.reshape(T // ROW_TILE, 1, ROW_TILE * TOP_K)

    xs = _dispatch(dest_tiles, h2p, D // 2 // LANES)
    ys = _gmm(meta, xs, w_gate, w_up, w_down)
    shared = _shared(h2b, ws_gate.astype(BF16), ws_up.astype(BF16), ws_down.astype(BF16))
    return _combine(dest_tiles, ys, gw_t.T, shared, x1, gt2, vec(g_ffn_post))


def kernel(x, c, positions, w_ada, b_ada, g_mix_pre, g_mix_post, g_ffn_pre, g_ffn_post, w_in, conv_w, ret_gn_w,
           w_out, w_router, router_bias, w_gate, w_up, w_down, ws_gate, ws_up, ws_down):
    B, S, D = x.shape
    assert B == 1, "one sequence per call"
    xt = x.reshape(S, D)
    pos_f = positions.astype(F32).reshape(S, 1)
    for l in range(w_ada.shape[0]):
        xt = _layer(xt, c, pos_f, w_ada[l], b_ada[l], g_mix_pre[l], g_mix_post[l], g_ffn_pre[l], g_ffn_post[l],
                    w_in[l], conv_w[l], ret_gn_w[l], w_out[l], w_router[l], router_bias[l], w_gate[l], w_up[l],
                    w_down[l], ws_gate[l], ws_up[l], ws_down[l])
    return xt.reshape(B, S, D)
```

```python
import functools
import math

import jax
import jax.numpy as jnp
from jax import lax
from jax.experimental import pallas as pl
from jax.experimental.pallas import tpu as pltpu

F32 = jnp.float32
BF16 = jnp.bfloat16
U32 = jnp.uint32
I32 = jnp.int32

CONV_K = 3
RET_HEAD_DIM = 128
ROPE_THETA = 10000.0
TOP_K = 8
N_GROUPS = 8
TOPK_GROUPS = 4
ROUTED_SCALE = 2.5
NORM_EPS = 1e-6
GN_EPS = 1e-5

LANES = 128
SUBLANES = 8
VMEM_LIMIT_CAP = 56 * 1024 * 1024

RET_CHUNK = 256
MOE_BLK = 128
ROW_TILE = 128
OUTPROJ_SUB = 256


def _cparams(n_axes, vmem_bytes):
    return pltpu.CompilerParams(
        dimension_semantics=("arbitrary",) * n_axes,
        vmem_limit_bytes=int(min(max(vmem_bytes, 16 * 1024 * 1024), VMEM_LIMIT_CAP)))


def _nbytes(shape, dtype):
    return math.prod(shape) * jnp.dtype(dtype).itemsize


def _dot(a, b):
    return jnp.dot(a, b, preferred_element_type=F32)


def _dot_nt(a, b):
    return lax.dot_general(a, b, (((1,), (1,)), ((), ())), preferred_element_type=F32)


def _dot_tn(a, b):
    return lax.dot_general(a, b, (((0,), (0,)), ((), ())), preferred_element_type=F32)


def _silu(v):
    return v * jax.nn.sigmoid(v)


def _pack_halves(lo, hi):
    lo_b = lax.bitcast_convert_type(lo.astype(BF16).astype(F32), U32)
    hi_b = lax.bitcast_convert_type(hi.astype(BF16).astype(F32), U32)
    return (lo_b >> 16) | (hi_b & jnp.uint32(0xFFFF0000))


def _unpack_halves(w):
    lo = lax.bitcast_convert_type(w << 16, F32)
    hi = lax.bitcast_convert_type(w & jnp.uint32(0xFFFF0000), F32)
    return lo, hi


def _store_row_tiles(ref, val):
    rows, S = val.shape[0], val.shape[1] // LANES
    for s in range(S):
        ref[pl.ds(s, rows, stride=S), :] = val[:, s * LANES:(s + 1) * LANES]


def _load_row_chunk(ref, s, S):
    return ref[pl.ds(s, ref.shape[0] // S, stride=S), :]


def _load_row_tiles(ref, S):
    return jnp.concatenate([_load_row_chunk(ref, s, S) for s in range(S)], axis=1)


def _ada_body(cb_ref, w_ref, b_ref, o_ref):
    s = _silu(cb_ref[...])
    for j in range(o_ref.shape[-1] // LANES):
        sl = slice(j * LANES, (j + 1) * LANES)
        o_ref[:, sl] = jnp.sum(w_ref[:, sl] * s, axis=0, keepdims=True) + b_ref[:, sl]


def _ada(c, w_ada, b_ada):
    D, N = w_ada.shape
    tn = min(1024, N)
    cb = jnp.broadcast_to(c.reshape(D, 1), (D, LANES))
    vmem = 2 * _nbytes((D, tn), F32) + 3 * _nbytes((D, LANES), F32) + (4 << 20)
    return pl.pallas_call(
        _ada_body,
        grid=(N // tn,),
        in_specs=[pl.BlockSpec((D, LANES), lambda j: (0, 0)),
                  pl.BlockSpec((D, tn), lambda j: (0, j)),
                  pl.BlockSpec((1, tn), lambda j: (0, j))],
        out_specs=pl.BlockSpec((1, tn), lambda j: (0, j)),
        out_shape=jax.ShapeDtypeStruct((1, N), F32),
        compiler_params=_cparams(1, vmem),
        name="ada",
    )(cb, w_ada, b_ada.reshape(1, N))


def _inproj_body(x_ref, g_ref, sc_ref, sh_ref, w_ref, o_ref, h_ref):
    @pl.when(pl.program_id(1) == 0)
    def _():
        xf = x_ref[...]
        r = xf * lax.rsqrt(jnp.mean(xf * xf, axis=-1, keepdims=True) + NORM_EPS)
        h_ref[...] = ((r * g_ref[...]) * (1.0 + sc_ref[...]) + sh_ref[...]).astype(BF16)

    o_ref[...] = _dot(h_ref[...], w_ref[...]).astype(o_ref.dtype)


def _inproj(x, g, sc, sh, w_bf, tn):
    T, D = x.shape
    N = w_bf.shape[1]
    tm = min(1024, T)
    vec = pl.BlockSpec((1, D), lambda i, j: (0, 0))
    vmem = (2 * _nbytes((tm, D), F32) + _nbytes((tm, D), BF16) + 2 * _nbytes((D, tn), BF16)
            + 2 * _nbytes((tm, tn), BF16) + 2 * _nbytes((tm, tn), F32) + 2 * _nbytes((tm, D), F32) + (4 << 20))
    return pl.pallas_call(
        _inproj_body,
        grid=(T // tm, N // tn),
        in_specs=[pl.BlockSpec((tm, D), lambda i, j: (i, 0)), vec, vec, vec,
                  pl.BlockSpec((D, tn), lambda i, j: (0, j))],
        out_specs=pl.BlockSpec((tm, tn), lambda i, j: (i, j)),
        out_shape=jax.ShapeDtypeStruct((T, N), BF16),
        scratch_shapes=[pltpu.VMEM((tm, D), BF16)],
        compiler_params=_cparams(2, vmem),
        name="inproj",
    )(x, g, sc, sh, w_bf)


def _conv_body(cb_ref, cc_ref, cx_ref, w_ref, o_ref, u_ref):
    tc = cb_ref.shape[0]

    @pl.when(pl.program_id(1) == 0)
    def _():
        u_ref[0:SUBLANES, :] = jnp.zeros((SUBLANES, u_ref.shape[1]), F32)

    u = cc_ref[...].astype(F32) * cx_ref[...].astype(F32)
    u_ref[SUBLANES:SUBLANES + tc, :] = u
    u1 = u_ref[SUBLANES - 1:SUBLANES - 1 + tc, :]
    u2 = u_ref[SUBLANES - 2:SUBLANES - 2 + tc, :]
    conv = u2 * w_ref[0:1, :] + u1 * w_ref[1:2, :] + u * w_ref[2:3, :]
    o_ref[...] = (cb_ref[...].astype(F32) * conv).astype(o_ref.dtype)
    u_ref[0:SUBLANES, :] = u_ref[tc:tc + SUBLANES, :]


def _conv(proj, conv_w, CW):
    T = proj.shape[0]
    tc = min(512, T)
    tw = min(512, CW)
    nb = CW // tw
    vmem = 8 * _nbytes((tc, tw), BF16) + 8 * _nbytes((tc + SUBLANES, tw), F32) + (4 << 20)
    return pl.pallas_call(
        _conv_body,
        grid=(nb, T // tc),
        in_specs=[pl.BlockSpec((tc, tw), lambda j, i: (i, j)),
                  pl.BlockSpec((tc, tw), lambda j, i: (i, nb + j)),
                  pl.BlockSpec((tc, tw), lambda j, i: (i, 2 * nb + j)),
                  pl.BlockSpec((CONV_K, tw), lambda j, i: (0, j))],
        out_specs=pl.BlockSpec((tc, tw), lambda j, i: (i, j)),
        out_shape=jax.ShapeDtypeStruct((T, CW), BF16),
        scratch_shapes=[pltpu.VMEM((tc + SUBLANES, tw), F32)],
        compiler_params=_cparams(2, vmem),
        name="conv",
    )(proj, proj, proj, conv_w)


def _ret_body(q_ref, k_ref, v_ref, g_ref, pos_ref, inv_ref, gnw_ref, o_ref, state_ref, dec_ref, *, H, C):
    dh = RET_HEAD_DIM
    log_gamma = [math.log1p(-(2.0 ** (-5.0 - h))) for h in range(H)]

    @pl.when(pl.program_id(0) == 0)
    def _():
        state_ref[...] = jnp.zeros_like(state_ref)
        rel = (lax.broadcasted_iota(I32, (C, C), 0) - lax.broadcasted_iota(I32, (C, C), 1)).astype(F32)
        for h in range(H):
            dec_ref[h] = jnp.where(rel >= 0.0, jnp.exp(log_gamma[h] * jnp.maximum(rel, 0.0)), 0.0)

    ang = pos_ref[...] * inv_ref[...]
    cos2 = jnp.cos(ang)
    sin2 = jnp.where(lax.broadcasted_iota(I32, (C, dh), 1) < dh // 2, -jnp.sin(ang), jnp.sin(ang))
    ridx = lax.broadcasted_iota(I32, (C, 1), 0).astype(F32)

    def rope(t):
        return t * cos2 + pltpu.roll(t, dh // 2, 1) * sin2

    for h in range(H):
        sl = slice(h * dh, (h + 1) * dh)
        lg = log_gamma[h]
        qr = rope(q_ref[:, sl].astype(F32))
        kr = rope(k_ref[:, sl].astype(F32)) * (dh ** -0.5)
        v = v_ref[:, sl]
        q_decay = jnp.exp(lg * (ridx + 1.0))
        k_decay = jnp.exp(lg * (C - 1.0 - ridx))
        scores = _dot_nt(qr.astype(BF16), kr.astype(BF16)) * dec_ref[h]
        state = state_ref[h]
        o = _dot(scores.astype(BF16), v) + _dot((qr * q_decay).astype(BF16), state.astype(BF16))
        state_ref[h] = state * math.exp(lg * C) + _dot_tn((kr * k_decay).astype(BF16), v)
        mu = jnp.mean(o, axis=-1, keepdims=True)
        d = o - mu
        var = jnp.mean(d * d, axis=-1, keepdims=True)
        on = d * lax.rsqrt(var + GN_EPS) * gnw_ref[:, sl]
        o_ref[:, sl] = (_silu(g_ref[:, sl].astype(F32)) * on).astype(o_ref.dtype)


def _retention(proj, pos_f, gn_w, RW, col0_blocks):
    T = proj.shape[0]
    C = min(RET_CHUNK, T)
    H = RW // RET_HEAD_DIM
    half = RET_HEAD_DIM // 2
    inv = ROPE_THETA ** (-jnp.arange(half, dtype=F32) / half)
    inv2 = jnp.concatenate([inv, inv]).reshape(1, RET_HEAD_DIM)
    blk = lambda off: pl.BlockSpec((C, RW), lambda i: (i, col0_blocks + off))
    vmem = (10 * _nbytes((C, RW), BF16) + _nbytes((H, RET_HEAD_DIM, RET_HEAD_DIM), F32)
            + _nbytes((H, C, C), F32) + 24 * _nbytes((C, C), F32) + (8 << 20))
    return pl.pallas_call(
        functools.partial(_ret_body, H=H, C=C),
        grid=(T // C,),
        in_specs=[blk(0), blk(1), blk(2), blk(3),
                  pl.BlockSpec((C, 1), lambda i: (i, 0)),
                  pl.BlockSpec((1, RET_HEAD_DIM), lambda i: (0, 0)),
                  pl.BlockSpec((1, RW), lambda i: (0, 0))],
        out_specs=pl.BlockSpec((C, RW), lambda i: (i, 0)),
        out_shape=jax.ShapeDtypeStruct((T, RW), BF16),
        scratch_shapes=[pltpu.VMEM((H, RET_HEAD_DIM, RET_HEAD_DIM), F32), pltpu.VMEM((H, C, C), F32)],
        compiler_params=_cparams(1, vmem),
        name="ret",
    )(proj, proj, proj, proj, pos_f, inv2, gn_w)


def _outproj_body(yc_ref, yr_ref, w_ref, x_ref, gpost_ref, gt_ref, gpre_ref, sc_ref, sh_ref, wr_ref,
                  x1_ref, h2b_ref, h2p_ref, lt_ref, wrhi_ref, wrlo_ref):
    CW = yc_ref.shape[1]
    D = x_ref.shape[1]

    @pl.when(pl.program_id(0) == 0)
    def _():
        w = wr_ref[...]
        hi = w.astype(BF16)
        wrhi_ref[...] = hi
        wrlo_ref[...] = (w - hi.astype(F32)).astype(BF16)

    S = D // 2 // LANES
    tm = x_ref.shape[0]
    sub = min(tm, OUTPROJ_SUB)
    for r0 in range(0, tm, sub):
        rows = slice(r0, r0 + sub)
        m = _dot(yc_ref[rows, :], w_ref[0:CW, :]) + _dot(yr_ref[rows, :], w_ref[CW:, :])
        mn = (m * lax.rsqrt(jnp.mean(m * m, axis=-1, keepdims=True) + NORM_EPS)) * gpost_ref[...]
        x1 = x_ref[rows, :] + gt_ref[...] * mn
        x1_ref[rows, :] = x1
        r = x1 * lax.rsqrt(jnp.mean(x1 * x1, axis=-1, keepdims=True) + NORM_EPS)
        h2 = (r * gpre_ref[...]) * (1.0 + sc_ref[...]) + sh_ref[...]
        hb = h2.astype(BF16)
        h2b_ref[rows, :] = hb
        _store_row_tiles(h2p_ref.at[pl.ds(r0 * S, sub * S)], _pack_halves(h2[:, :D // 2], h2[:, D // 2:]))
        hl = (h2 - hb.astype(F32)).astype(BF16)
        lt_ref[:, rows] = (_dot_nt(wrhi_ref[...], hb) + _dot_nt(wrhi_ref[...], hl)) + _dot_nt(wrlo_ref[...], hb)


def _outproj(yc, yr, w_out_bf, x, gpost, gt, gpre, sc, sh, w_router):
    T, D = x.shape
    CW, RW = yc.shape[1], yr.shape[1]
    E = w_router.shape[0]
    S = D // 2 // LANES
    tm = min(256, T)
    vec = pl.BlockSpec((1, D), lambda i: (0, 0))
    row = lambda w: pl.BlockSpec((tm, w), lambda i: (i, 0))
    vmem = (2 * _nbytes((CW + RW, D), BF16) + 2 * _nbytes((E, D), F32) + 2 * _nbytes((E, D), BF16)
            + 4 * _nbytes((tm, D), F32) + 4 * _nbytes((tm, D), BF16) + 2 * _nbytes((tm, D // 2), U32)
            + 2 * _nbytes((E, tm), F32) + 8 * _nbytes((tm, D), F32) + (4 << 20))
    return pl.pallas_call(
        _outproj_body,
        grid=(T // tm,),
        in_specs=[row(CW), row(RW), pl.BlockSpec((CW + RW, D), lambda i: (0, 0)), row(D),
                  vec, vec, vec, vec, vec, pl.BlockSpec((E, D), lambda i: (0, 0))],
        out_specs=[row(D), row(D), pl.BlockSpec((tm * S, LANES), lambda i: (i, 0)),
                   pl.BlockSpec((E, tm), lambda i: (0, i))],
        out_shape=[jax.ShapeDtypeStruct((T, D), F32), jax.ShapeDtypeStruct((T, D), BF16),
                   jax.ShapeDtypeStruct((T * S, LANES), U32), jax.ShapeDtypeStruct((E, T), F32)],
        scratch_shapes=[pltpu.VMEM((E, D), BF16), pltpu.VMEM((E, D), BF16)],
        compiler_params=_cparams(1, vmem),
        name="outproj",
    )(yc, yr, w_out_bf, x, gpost, gt, gpre, sc, sh, w_router)


def _route_body(lt_ref, bias_ref, eidx_ref, gw_ref, rank_ref, cnt_ref, carry_ref):
    E, TL = lt_ref.shape
    G, EG = N_GROUPS, E // N_GROUPS
    NEG, BIG = -jnp.inf, 1e9

    @pl.when(pl.program_id(0) == 0)
    def _():
        carry_ref[...] = jnp.zeros_like(carry_ref)

    scores = jax.nn.sigmoid(lt_ref[...]).reshape(G, EG, TL)
    biased = scores + bias_ref[...].reshape(G, EG, 1)
    eio = lax.broadcasted_iota(I32, (G, EG, TL), 1).astype(F32)
    eid = lax.broadcasted_iota(I32, (G, EG, TL), 0).astype(F32) * EG + eio
    gio = lax.broadcasted_iota(I32, (G, 1, TL), 0).astype(F32)

    m1 = jnp.max(biased, axis=1, keepdims=True)
    i1 = jnp.min(jnp.where(biased == m1, eio, BIG), axis=1, keepdims=True)
    m2 = jnp.max(jnp.where(eio == i1, NEG, biased), axis=1, keepdims=True)
    cur = m1 + m2
    gmask = jnp.zeros((G, 1, TL), F32)
    for _ in range(TOPK_GROUPS):
        mm = jnp.max(cur, axis=0, keepdims=True)
        ii = jnp.min(jnp.where(cur == mm, gio, BIG), axis=0, keepdims=True)
        sel = gio == ii
        gmask = jnp.where(sel, 1.0, gmask)
        cur = jnp.where(sel, NEG, cur)

    cur = jnp.where(gmask > 0.0, biased, NEG)
    sels, idxs, ws = [], [], []
    for _ in range(TOP_K):
        mm = jnp.max(jnp.max(cur, axis=1, keepdims=True), axis=0, keepdims=True)
        cand = jnp.where(cur == mm, eid, BIG)
        ii = jnp.min(jnp.min(cand, axis=1, keepdims=True), axis=0, keepdims=True)
        sel = eid == ii
        w = jnp.where(sel, scores, 0.0)
        ws.append(jnp.sum(jnp.sum(w, axis=1, keepdims=True), axis=0, keepdims=True).reshape(1, TL))
        idxs.append(ii.reshape(1, TL))
        sels.append(sel)
        cur = jnp.where(sel, NEG, cur)

    wsum = ws[0]
    for w in ws[1:]:
        wsum = wsum + w
    for k in range(TOP_K):
        eidx_ref[k:k + 1, :] = idxs[k].astype(I32)
        gw_ref[k:k + 1, :] = ws[k] / wsum * ROUTED_SCALE

    chosen = jnp.zeros((G, EG, TL), F32)
    for sel in sels:
        chosen = jnp.where(sel, 1.0, chosen)
    chosen2 = chosen.reshape(E, TL).astype(BF16)
    before = (lax.broadcasted_iota(I32, (TL, TL), 0) < lax.broadcasted_iota(I32, (TL, TL), 1))
    prefix = _dot(chosen2, jnp.where(before, 1.0, 0.0).astype(BF16))
    total = _dot(chosen2, jnp.ones((TL, TL), BF16))
    carry = carry_ref[...]
    rank_full = (carry + prefix).reshape(G, EG, TL)
    for k in range(TOP_K):
        r = jnp.where(sels[k], rank_full, 0.0)
        r = jnp.sum(jnp.sum(r, axis=1, keepdims=True), axis=0, keepdims=True).reshape(1, TL)
        rank_ref[k:k + 1, :] = r.astype(I32)
    carry = carry + total
    carry_ref[...] = carry
    cnt_ref[...] = carry[:, 0:LANES]


def _route(logits_t, bias):
    E, T = logits_t.shape
    TL = min(256, T)
    tok = pl.BlockSpec((TOP_K, TL), lambda i: (0, i))
    vmem = 64 * _nbytes((E, TL), F32) + (8 << 20)
    return pl.pallas_call(
        _route_body,
        grid=(T // TL,),
        in_specs=[pl.BlockSpec((E, TL), lambda i: (0, i)), pl.BlockSpec((E, 1), lambda i: (0, 0))],
        out_specs=[tok, tok, tok, pl.BlockSpec((E, LANES), lambda i: (0, 0))],
        out_shape=[jax.ShapeDtypeStruct((TOP_K, T), I32), jax.ShapeDtypeStruct((TOP_K, T), F32),
                   jax.ShapeDtypeStruct((TOP_K, T), I32), jax.ShapeDtypeStruct((E, LANES), F32)],
        scratch_shapes=[pltpu.VMEM((E, TL), F32)],
        compiler_params=_cparams(1, vmem),
        name="route",
    )(logits_t, bias.reshape(E, 1))


def _dest_body(eidx_ref, rank_ref, starts_ref, dest_ref):
    E, TL = starts_ref.shape[0], eidx_ref.shape[1]
    eid = lax.broadcasted_iota(I32, (E, TL), 0)
    starts = starts_ref[...]
    for k in range(TOP_K):
        base = jnp.sum(jnp.where(eid == eidx_ref[k:k + 1, :], starts, 0.0), axis=0, keepdims=True)
        dest_ref[k:k + 1, :] = base.astype(I32) + rank_ref[k:k + 1, :]


def _dest(eidx_t, rank_t, starts):
    K, T = eidx_t.shape
    E = starts.shape[0]
    TL = min(512, T)
    tok = pl.BlockSpec((K, TL), lambda i: (0, i))
    return pl.pallas_call(
        _dest_body,
        grid=(T // TL,),
        in_specs=[tok, tok, pl.BlockSpec((E, 1), lambda i: (0, 0))],
        out_specs=tok,
        out_shape=jax.ShapeDtypeStruct((K, T), I32),
        compiler_params=_cparams(1, 16 * _nbytes((E, TL), F32) + (4 << 20)),
        name="dest",
    )(eidx_t, rank_t, starts.astype(F32).reshape(E, 1))


def _row_span(r, S):
    if isinstance(r, int):
        return pl.ds(r * S, S)
    return pl.ds(pl.multiple_of(r * S, S), S)


def _row_copy(src, s, dst, d, sem, S):
    return pltpu.make_async_copy(src.at[_row_span(s, S)], dst.at[_row_span(d, S)], sem)


def _dispatch_body(dest_ref, h_hbm, xs_hbm, hbuf_ref, load_sems, sems, *, S):
    i = pl.program_id(0)
    last = pl.num_programs(0) - 1
    slot = i % 2
    tile_rows = ROW_TILE * S

    def load(tile, dst_slot):
        src = h_hbm.at[pl.ds(pl.multiple_of(tile * tile_rows, tile_rows), tile_rows)]
        return pltpu.make_async_copy(src, hbuf_ref.at[dst_slot], load_sems.at[dst_slot])

    def drain(s):
        for _ in range(ROW_TILE * TOP_K):
            _row_copy(hbuf_ref.at[s], 0, xs_hbm, 0, sems.at[s], S).wait()

    @pl.when(i == 0)
    def _():
        load(0, 0).start()

    @pl.when(i > 0)
    def _():
        drain(1 - slot)

    @pl.when(i < last)
    def _():
        load(i + 1, 1 - slot).start()

    load(i, slot).wait()
    for t in range(ROW_TILE):
        for k in range(TOP_K):
            _row_copy(hbuf_ref.at[slot], t, xs_hbm, dest_ref[0, 0, t * TOP_K + k], sems.at[slot],
                      S).start(priority=k % 2)

    @pl.when(i == last)
    def _():
        drain(slot)


def _dispatch(dest_tiles, h2p, S):
    T = h2p.shape[0] // S
    n_tiles = T // ROW_TILE
    return pl.pallas_call(
        functools.partial(_dispatch_body, S=S),
        grid=(n_tiles,),
        in_specs=[pl.BlockSpec((1, 1, ROW_TILE * TOP_K), lambda i: (i, 0, 0), memory_space=pltpu.SMEM),
                  pl.BlockSpec(memory_space=pl.ANY)],
        out_specs=pl.BlockSpec(memory_space=pl.ANY),
        out_shape=jax.ShapeDtypeStruct((T * TOP_K * S, LANES), U32),
        scratch_shapes=[pltpu.VMEM((2, ROW_TILE * S, LANES), U32), pltpu.SemaphoreType.DMA((2,)),
                        pltpu.SemaphoreType.DMA((2,))],
        compiler_params=_cparams(1, 16 << 20),
        name="dispatch",
    )(dest_tiles, h2p)


def _gmm_body(tile_ref, grp_ref, lo_ref, hi_ref, tfirst_ref, gfirst_ref, gslot_ref, gnext_ref, gnext2_ref,
              xs_ref, wg_hbm, wu_hbm, wd_hbm, ys_ref,
              wgf_ref, wuf_ref, wdf_ref, wgb_ref, wub_ref, wdb_ref, sems):
    v = pl.program_id(0)
    half = wgb_ref.shape[0] // 2
    S = half // LANES
    landing = ((wg_hbm, wgf_ref, wgb_ref), (wu_hbm, wuf_ref, wub_ref), (wd_hbm, wdf_ref, wdb_ref))

    def fetch(j, g, slot):
        return pltpu.make_async_copy(landing[j][0].at[g], landing[j][1].at[slot], sems.at[slot, j])

    @pl.when(v == 0)
    def _():
        for j in range(3):
            fetch(j, grp_ref[0], 0).start()

        @pl.when(gnext_ref[0] >= 0)
        def _():
            for j in range(3):
                fetch(j, gnext_ref[0], 1).start()

    @pl.when(gfirst_ref[v] == 1)
    def _():
        slot = gslot_ref[v]
        nxt2 = gnext2_ref[v]
        for j in range(3):
            fetch(j, grp_ref[v], slot).wait()
            landing[j][2][...] = landing[j][1][slot].astype(BF16)

            @pl.when(nxt2 >= 0)
            def _():
                fetch(j, nxt2, slot).start()

    lo, hi = lo_ref[v], hi_ref[v]

    @pl.when(hi > lo)
    def _():
        xl, xh = _unpack_halves(_load_row_tiles(xs_ref, S))
        xl, xh = xl.astype(BF16), xh.astype(BF16)
        gate = _dot(xl, wgb_ref[0:half, :]) + _dot(xh, wgb_ref[half:, :])
        up = _dot(xl, wub_ref[0:half, :]) + _dot(xh, wub_ref[half:, :])
        act = (_silu(gate) * up).astype(BF16)
        y = _dot(act, wdb_ref[...])
        packed = _pack_halves(y[:, :half], y[:, half:])
        rows = lax.broadcasted_iota(I32, packed.shape, 0)
        mine = (rows >= lo) & (rows < hi)

        @pl.when(tfirst_ref[v] == 1)
        def _():
            _store_row_tiles(ys_ref, jnp.where(mine, packed, jnp.uint32(0)))

        @pl.when(tfirst_ref[v] == 0)
        def _():
            _store_row_tiles(ys_ref, jnp.where(mine, packed, _load_row_tiles(ys_ref, S)))


def _gmm(meta, xs, w_gate, w_up, w_down):
    E, D, FF = w_gate.shape
    half = D // 2
    S = half // LANES
    NV = meta[0].shape[0]
    n_meta = len(meta)
    rows = pl.BlockSpec((MOE_BLK * S, LANES), lambda v, tile, *_: (tile[v], 0))
    hbm = pl.BlockSpec(memory_space=pl.ANY)
    vmem = (6 * _nbytes((D, FF), F32) + 3 * _nbytes((D, FF), BF16) + 4 * _nbytes((MOE_BLK, half), U32)
            + 8 * _nbytes((MOE_BLK, D), F32) + (6 << 20))
    return pl.pallas_call(
        _gmm_body,
        grid_spec=pltpu.PrefetchScalarGridSpec(
            num_scalar_prefetch=n_meta,
            grid=(NV,),
            in_specs=[rows, hbm, hbm, hbm],
            out_specs=rows,
            scratch_shapes=[pltpu.VMEM((2, D, FF), F32), pltpu.VMEM((2, D, FF), F32), pltpu.VMEM((2, FF, D), F32),
                            pltpu.VMEM((D, FF), BF16), pltpu.VMEM((D, FF), BF16), pltpu.VMEM((FF, D), BF16),
                            pltpu.SemaphoreType.DMA((2, 3))]),
        out_shape=jax.ShapeDtypeStruct(xs.shape, U32),
        compiler_params=_cparams(1, vmem),
        name="gmm",
    )(*meta, xs, w_gate, w_up, w_down)


def _gmm_meta(counts, n_rows):
    E = counts.shape[0]
    n_tiles = n_rows // MOE_BLK
    NV = n_tiles + E - 1
    eids = jnp.arange(E, dtype=I32)
    ends = jnp.cumsum(counts)
    starts = ends - counts
    first_tile = starts // MOE_BLK
    last_tile = jnp.maximum(ends - 1, 0) // MOE_BLK
    n_vis = jnp.where(counts > 0, last_tile - first_tile + 1, 0)
    v_end = jnp.cumsum(n_vis)
    v_start = v_end - n_vis
    total = v_end[-1]
    later = jnp.where((eids[None, :] > eids[:, None]) & (counts[None, :] > 0), eids[None, :], E)
    nxt = jnp.min(later, axis=1)
    nxt2 = jnp.min(jnp.where(eids[None, :] > nxt[:, None], later, E), axis=1)
    nxt = jnp.where(nxt == E, -1, nxt)
    nxt2 = jnp.where(nxt2 == E, -1, nxt2)
    order = jnp.cumsum((counts > 0).astype(I32)) - 1
    v = jnp.arange(NV, dtype=I32)
    vc = jnp.minimum(v, total - 1)
    grp = jnp.sum((v_end[None, :] <= vc[:, None]).astype(I32), axis=1)
    onehot = grp[:, None] == eids[None, :]
    pick = lambda a: jnp.sum(jnp.where(onehot, a[None, :], 0), axis=1)
    tile = pick(first_tile) + (vc - pick(v_start))
    valid = v < total
    lo = jnp.where(valid, jnp.maximum(pick(starts), tile * MOE_BLK) - tile * MOE_BLK, 0)
    hi = jnp.where(valid, jnp.minimum(pick(ends), (tile + 1) * MOE_BLK) - tile * MOE_BLK, 0)
    prev = lambda a: jnp.concatenate([jnp.full((1,), -1, I32), a[:-1]])
    tfirst = (tile != prev(tile)) & valid
    gfirst = (grp != prev(grp)) & valid
    meta = (tile, grp, lo, hi, tfirst, gfirst, pick(order) % 2, pick(nxt), pick(nxt2))
    return tuple(m.astype(I32) for m in meta), starts


def _shared_body(h_ref, wg_ref, wu_ref, wd_ref, o_ref):
    h = h_ref[...]
    act = (_silu(_dot(h, wg_ref[...])) * _dot(h, wu_ref[...])).astype(BF16)
    o_ref[...] = _dot(act, wd_ref[...]).astype(o_ref.dtype)


def _shared(h2b, wg_bf, wu_bf, wd_bf):
    T, D = h2b.shape
    SF = wg_bf.shape[1]
    tm = min(512, T)
    full = lambda a, b: pl.BlockSpec((a, b), lambda i: (0, 0))
    vmem = 6 * _nbytes((D, SF), BF16) + 4 * _nbytes((tm, D), BF16) + 3 * _nbytes((tm, D), F32) + (4 << 20)
    return pl.pallas_call(
        _shared_body,
        grid=(T // tm,),
        in_specs=[pl.BlockSpec((tm, D), lambda i: (i, 0)), full(D, SF), full(D, SF), full(SF, D)],
        out_specs=pl.BlockSpec((tm, D), lambda i: (i, 0)),
        out_shape=jax.ShapeDtypeStruct((T, D), BF16),
        compiler_params=_cparams(1, vmem),
        name="shared",
    )(h2b, wg_bf, wu_bf, wd_bf)


def _combine_body(dest0_ref, destn_ref, ys_hbm, gw_ref, sh_ref, x1_ref, gt_ref, gpost_ref, o_ref, ybuf_ref, sems):
    i = pl.program_id(0)
    last = pl.num_programs(0) - 1
    D = x1_ref.shape[1]
    half = D // 2
    S = half // LANES

    def gather(dest_ref, slot):
        for t in range(ROW_TILE):
            for k in range(TOP_K):
                _row_copy(ys_hbm, dest_ref[0, 0, t * TOP_K + k], ybuf_ref.at[slot, k], t,
                          sems.at[slot], S).start(priority=k % 2)

    def drain(slot):
        for _ in range(ROW_TILE * TOP_K):
            _row_copy(ys_hbm, 0, ybuf_ref.at[slot, 0], 0, sems.at[slot], S).wait()

    @pl.when(i == 0)
    def _():
        gather(dest0_ref, 0)

    gather(destn_ref, (i + 1) % 2)
    drain(i % 2)
    slot = i % 2

    wb = [jnp.broadcast_to(gw_ref[:, k:k + 1], (ROW_TILE, LANES)) for k in range(TOP_K)]
    ssq = jnp.zeros((ROW_TILE, 1), F32)
    for s in range(S):
        c_lo = slice(s * LANES, (s + 1) * LANES)
        c_hi = slice(half + s * LANES, half + (s + 1) * LANES)
        fl = sh_ref[:, c_lo].astype(F32)
        fh = sh_ref[:, c_hi].astype(F32)
        for k in range(TOP_K):
            yl, yh = _unpack_halves(_load_row_chunk(ybuf_ref.at[slot, k], s, S))
            fl = fl + yl * wb[k]
            fh = fh + yh * wb[k]
        ssq = ssq + jnp.sum(fl * fl, axis=-1, keepdims=True) + jnp.sum(fh * fh, axis=-1, keepdims=True)
        o_ref[:, c_lo] = fl
        o_ref[:, c_hi] = fh
    r = lax.rsqrt(ssq / D + NORM_EPS)
    o_ref[...] = x1_ref[...] + gt_ref[...] * ((o_ref[...] * r) * gpost_ref[...])

    @pl.when(i == last)
    def _():
        drain((i + 1) % 2)


def _combine(dest_tiles, ys, gw, shared, x1, gt, gpost):
    T, D = x1.shape
    half = D // 2
    n_tiles = T // ROW_TILE
    row = lambda w: pl.BlockSpec((ROW_TILE, w), lambda i: (i, 0))
    vec = pl.BlockSpec((1, D), lambda i: (0, 0))
    idx = lambda f: pl.BlockSpec((1, 1, ROW_TILE * TOP_K), f, memory_space=pltpu.SMEM)
    S = half // LANES
    vmem = (2 * _nbytes((TOP_K, ROW_TILE, half), U32) + 4 * _nbytes((ROW_TILE, D), F32)
            + 2 * _nbytes((ROW_TILE, D), BF16) + 8 * _nbytes((ROW_TILE, D), F32) + (4 << 20))
    return pl.pallas_call(
        _combine_body,
        grid=(n_tiles,),
        in_specs=[idx(lambda i: (0, 0, 0)), idx(lambda i: (jnp.minimum(i + 1, n_tiles - 1), 0, 0)),
                  pl.BlockSpec(memory_space=pl.ANY), row(TOP_K), row(D), row(D), vec, vec],
        out_specs=row(D),
        out_shape=jax.ShapeDtypeStruct((T, D), F32),
        scratch_shapes=[pltpu.VMEM((2, TOP_K, ROW_TILE * S, LANES), U32), pltpu.SemaphoreType.DMA((2,))],
        compiler_params=_cparams(1, vmem),
        name="combine",
    )(dest_tiles, dest_tiles, ys, gw, shared, x1, gt, gpost)


def _layer(x, c, pos_f, w_ada, b_ada, g_mix_pre, g_mix_post, g_ffn_pre, g_ffn_post, w_in, conv_w, ret_gn_w,
           w_out, w_router, router_bias, w_gate, w_up, w_down, ws_gate, ws_up, ws_down):
    T, D = x.shape
    CW = conv_w.shape[-1]
    RW = ret_gn_w.shape[-1]
    assert CW == RW and w_in.shape[1] == 3 * CW + 4 * RW and w_out.shape[0] == CW + RW
    assert T % ROW_TILE == 0 and (T * TOP_K) % MOE_BLK == 0 and D % (2 * LANES) == 0
    vec = lambda a: a.reshape(1, D)

    mod = _ada(c, w_ada, b_ada)
    sh1, sc1, gt1, sh2, sc2, gt2 = (mod[:, i * D:(i + 1) * D] for i in range(6))

    proj = _inproj(x, vec(g_mix_pre), sc1, sh1, w_in.astype(BF16), tn=CW)
    y_conv = _conv(proj, conv_w, CW)
    y_ret = _retention(proj, pos_f, ret_gn_w.reshape(1, RW), RW, col0_blocks=3)
    x1, h2b, h2p, logits_t = _outproj(y_conv, y_ret, w_out.astype(BF16), x, vec(g_mix_post), gt1,
                                      vec(g_ffn_pre), sc2, sh2, w_router)

    eidx_t, gw_t, rank_t, cnt = _route(logits_t, router_bias)
    counts = cnt[:, 0].astype(I32)
    meta, starts = _gmm_meta(counts, T * TOP_K)
    dest = _dest(eidx_t, rank_t, starts).T
    dest_tiles = dest.reshape(T // ROW_TILE, 1, ROW_TILE * TOP_K)

    xs = _dispatch(dest_tiles, h2p, D // 2 // LANES)
    ys = _gmm(meta, xs, w_gate, w_up, w_down)
    shared = _shared(h2b, ws_gate.astype(BF16), ws_up.astype(BF16), ws_down.astype(BF16))
    return _combine(dest_tiles, ys, gw_t.T, shared, x1, gt2, vec(g_ffn_post))


def kernel(x, c, positions, w_ada, b_ada, g_mix_pre, g_mix_post, g_ffn_pre, g_ffn_post, w_in, conv_w, ret_gn_w,
           w_out, w_router, router_bias, w_gate, w_up, w_down, ws_gate, ws_up, ws_down):
    B, S, D = x.shape
    assert B == 1, "one sequence per call"
    xt = x.reshape(S, D)
    pos_f = positions.astype(F32).reshape(S, 1)
    for l in range(w_ada.shape[0]):
        xt = _layer(xt, c, pos_f, w_ada[l], b_ada[l], g_mix_pre[l], g_mix_post[l], g_ffn_pre[l], g_ffn_post[l],
                    w_in[l], conv_w[l], ret_gn_w[l], w_out[l], w_router[l], router_bias[l], w_gate[l], w_up[l],
                    w_down[l], ws_gate[l], ws_up[l], ws_down[l])
    return xt.reshape(B, S, D)
```

```python
import functools
import math

import jax
import jax.numpy as jnp
from jax import lax
from jax.experimental import pallas as pl
from jax.experimental.pallas import tpu as pltpu

F32 = jnp.float32
BF16 = jnp.bfloat16
U32 = jnp.uint32
I32 = jnp.int32

CONV_K = 3
RET_HEAD_DIM = 128
ROPE_THETA = 10000.0
TOP_K = 8
N_GROUPS = 8
TOPK_GROUPS = 4
ROUTED_SCALE = 2.5
NORM_EPS = 1e-6
GN_EPS = 1e-5

LANES = 128
SUBLANES = 8
VMEM_LIMIT_CAP = 56 * 1024 * 1024

RET_CHUNK = 256
MOE_BLK = 128
GMM_MAX_SUB = 4
ROW_TILE = 128
OUTPROJ_SUB = 256


def _cparams(n_axes, vmem_bytes):
    return pltpu.CompilerParams(
        dimension_semantics=("arbitrary",) * n_axes,
        vmem_limit_bytes=int(min(max(vmem_bytes, 16 * 1024 * 1024), VMEM_LIMIT_CAP)))


def _nbytes(shape, dtype):
    return math.prod(shape) * jnp.dtype(dtype).itemsize


def _dot(a, b):
    return jnp.dot(a, b, preferred_element_type=F32)


def _dot_nt(a, b):
    return lax.dot_general(a, b, (((1,), (1,)), ((), ())), preferred_element_type=F32)


def _dot_tn(a, b):
    return lax.dot_general(a, b, (((0,), (0,)), ((), ())), preferred_element_type=F32)


def _silu(v):
    return v * jax.nn.sigmoid(v)


def _pack_halves(lo, hi):
    lo_b = lax.bitcast_convert_type(lo.astype(BF16).astype(F32), U32)
    hi_b = lax.bitcast_convert_type(hi.astype(BF16).astype(F32), U32)
    return (lo_b >> 16) | (hi_b & jnp.uint32(0xFFFF0000))


def _unpack_halves(w):
    lo = lax.bitcast_convert_type(w << 16, F32)
    hi = lax.bitcast_convert_type(w & jnp.uint32(0xFFFF0000), F32)
    return lo, hi


def _store_row_tiles(ref, val):
    rows, S = val.shape[0], val.shape[1] // LANES
    for s in range(S):
        ref[pl.ds(s, rows, stride=S), :] = val[:, s * LANES:(s + 1) * LANES]


def _load_row_chunk(ref, s, S):
    return ref[pl.ds(s, ref.shape[0] // S, stride=S), :]


def _load_row_tiles(ref, S):
    return jnp.concatenate([_load_row_chunk(ref, s, S) for s in range(S)], axis=1)


def _ada_body(cb_ref, w_ref, b_ref, o_ref):
    s = _silu(cb_ref[...])
    for j in range(o_ref.shape[-1] // LANES):
        sl = slice(j * LANES, (j + 1) * LANES)
        o_ref[:, sl] = jnp.sum(w_ref[:, sl] * s, axis=0, keepdims=True) + b_ref[:, sl]


def _ada(c, w_ada, b_ada):
    D, N = w_ada.shape
    tn = min(1024, N)
    cb = jnp.broadcast_to(c.reshape(D, 1), (D, LANES))
    vmem = 2 * _nbytes((D, tn), F32) + 3 * _nbytes((D, LANES), F32) + (4 << 20)
    return pl.pallas_call(
        _ada_body,
        grid=(N // tn,),
        in_specs=[pl.BlockSpec((D, LANES), lambda j: (0, 0)),
                  pl.BlockSpec((D, tn), lambda j: (0, j)),
                  pl.BlockSpec((1, tn), lambda j: (0, j))],
        out_specs=pl.BlockSpec((1, tn), lambda j: (0, j)),
        out_shape=jax.ShapeDtypeStruct((1, N), F32),
        compiler_params=_cparams(1, vmem),
        name="ada",
    )(cb, w_ada, b_ada.reshape(1, N))


def _inproj_body(x_ref, g_ref, sc_ref, sh_ref, w_ref, o_ref, h_ref):
    @pl.when(pl.program_id(1) == 0)
    def _():
        xf = x_ref[...]
        r = xf * lax.rsqrt(jnp.mean(xf * xf, axis=-1, keepdims=True) + NORM_EPS)
        h_ref[...] = ((r * g_ref[...]) * (1.0 + sc_ref[...]) + sh_ref[...]).astype(BF16)

    o_ref[...] = _dot(h_ref[...], w_ref[...]).astype(o_ref.dtype)


def _inproj(x, g, sc, sh, w_bf, tn):
    T, D = x.shape
    N = w_bf.shape[1]
    tm = min(1024, T)
    vec = pl.BlockSpec((1, D), lambda i, j: (0, 0))
    vmem = (2 * _nbytes((tm, D), F32) + _nbytes((tm, D), BF16) + 2 * _nbytes((D, tn), BF16)
            + 2 * _nbytes((tm, tn), BF16) + 2 * _nbytes((tm, tn), F32) + 2 * _nbytes((tm, D), F32) + (4 << 20))
    return pl.pallas_call(
        _inproj_body,
        grid=(T // tm, N // tn),
        in_specs=[pl.BlockSpec((tm, D), lambda i, j: (i, 0)), vec, vec, vec,
                  pl.BlockSpec((D, tn), lambda i, j: (0, j))],
        out_specs=pl.BlockSpec((tm, tn), lambda i, j: (i, j)),
        out_shape=jax.ShapeDtypeStruct((T, N), BF16),
        scratch_shapes=[pltpu.VMEM((tm, D), BF16)],
        compiler_params=_cparams(2, vmem),
        name="inproj",
    )(x, g, sc, sh, w_bf)


def _conv_body(cb_ref, cc_ref, cx_ref, w_ref, o_ref, u_ref):
    tc = cb_ref.shape[0]

    @pl.when(pl.program_id(1) == 0)
    def _():
        u_ref[0:SUBLANES, :] = jnp.zeros((SUBLANES, u_ref.shape[1]), F32)

    u = cc_ref[...].astype(F32) * cx_ref[...].astype(F32)
    u_ref[SUBLANES:SUBLANES + tc, :] = u
    u1 = u_ref[SUBLANES - 1:SUBLANES - 1 + tc, :]
    u2 = u_ref[SUBLANES - 2:SUBLANES - 2 + tc, :]
    conv = u2 * w_ref[0:1, :] + u1 * w_ref[1:2, :] + u * w_ref[2:3, :]
    o_ref[...] = (cb_ref[...].astype(F32) * conv).astype(o_ref.dtype)
    u_ref[0:SUBLANES, :] = u_ref[tc:tc + SUBLANES, :]


def _conv(proj, conv_w, CW):
    T = proj.shape[0]
    tc = min(512, T)
    tw = min(512, CW)
    nb = CW // tw
    vmem = 8 * _nbytes((tc, tw), BF16) + 8 * _nbytes((tc + SUBLANES, tw), F32) + (4 << 20)
    return pl.pallas_call(
        _conv_body,
        grid=(nb, T // tc),
        in_specs=[pl.BlockSpec((tc, tw), lambda j, i: (i, j)),
                  pl.BlockSpec((tc, tw), lambda j, i: (i, nb + j)),
                  pl.BlockSpec((tc, tw), lambda j, i: (i, 2 * nb + j)),
                  pl.BlockSpec((CONV_K, tw), lambda j, i: (0, j))],
        out_specs=pl.BlockSpec((tc, tw), lambda j, i: (i, j)),
        out_shape=jax.ShapeDtypeStruct((T, CW), BF16),
        scratch_shapes=[pltpu.VMEM((tc + SUBLANES, tw), F32)],
        compiler_params=_cparams(2, vmem),
        name="conv",
    )(proj, proj, proj, conv_w)


def _ret_body(q_ref, k_ref, v_ref, g_ref, pos_ref, inv_ref, gnw_ref, o_ref, state_ref, dec_ref, *, H, C):
    dh = RET_HEAD_DIM
    log_gamma = [math.log1p(-(2.0 ** (-5.0 - h))) for h in range(H)]

    @pl.when(pl.program_id(0) == 0)
    def _():
        state_ref[...] = jnp.zeros_like(state_ref)
        rel = (lax.broadcasted_iota(I32, (C, C), 0) - lax.broadcasted_iota(I32, (C, C), 1)).astype(F32)
        for h in range(H):
            dec_ref[h] = jnp.where(rel >= 0.0, jnp.exp(log_gamma[h] * jnp.maximum(rel, 0.0)), 0.0)

    ang = pos_ref[...] * inv_ref[...]
    cos2 = jnp.cos(ang)
    sin2 = jnp.where(lax.broadcasted_iota(I32, (C, dh), 1) < dh // 2, -jnp.sin(ang), jnp.sin(ang))
    ridx = lax.broadcasted_iota(I32, (C, 1), 0).astype(F32)

    def rope(t):
        return t * cos2 + pltpu.roll(t, dh // 2, 1) * sin2

    for h in range(H):
        sl = slice(h * dh, (h + 1) * dh)
        lg = log_gamma[h]
        qr = rope(q_ref[:, sl].astype(F32))
        kr = rope(k_ref[:, sl].astype(F32)) * (dh ** -0.5)
        v = v_ref[:, sl]
        q_decay = jnp.exp(lg * (ridx + 1.0))
        k_decay = jnp.exp(lg * (C - 1.0 - ridx))
        scores = _dot_nt(qr.astype(BF16), kr.astype(BF16)) * dec_ref[h]
        state = state_ref[h]
        o = _dot(scores.astype(BF16), v) + _dot((qr * q_decay).astype(BF16), state.astype(BF16))
        state_ref[h] = state * math.exp(lg * C) + _dot_tn((kr * k_decay).astype(BF16), v)
        mu = jnp.mean(o, axis=-1, keepdims=True)
        d = o - mu
        var = jnp.mean(d * d, axis=-1, keepdims=True)
        on = d * lax.rsqrt(var + GN_EPS) * gnw_ref[:, sl]
        o_ref[:, sl] = (_silu(g_ref[:, sl].astype(F32)) * on).astype(o_ref.dtype)


def _retention(proj, pos_f, gn_w, RW, col0_blocks):
    T = proj.shape[0]
    C = min(RET_CHUNK, T)
    H = RW // RET_HEAD_DIM
    half = RET_HEAD_DIM // 2
    inv = ROPE_THETA ** (-jnp.arange(half, dtype=F32) / half)
    inv2 = jnp.concatenate([inv, inv]).reshape(1, RET_HEAD_DIM)
    blk = lambda off: pl.BlockSpec((C, RW), lambda i: (i, col0_blocks + off))
    vmem = (10 * _nbytes((C, RW), BF16) + _nbytes((H, RET_HEAD_DIM, RET_HEAD_DIM), F32)
            + _nbytes((H, C, C), F32) + 24 * _nbytes((C, C), F32) + (8 << 20))
    return pl.pallas_call(
        functools.partial(_ret_body, H=H, C=C),
        grid=(T // C,),
        in_specs=[blk(0), blk(1), blk(2), blk(3),
                  pl.BlockSpec((C, 1), lambda i: (i, 0)),
                  pl.BlockSpec((1, RET_HEAD_DIM), lambda i: (0, 0)),
                  pl.BlockSpec((1, RW), lambda i: (0, 0))],
        out_specs=pl.BlockSpec((C, RW), lambda i: (i, 0)),
        out_shape=jax.ShapeDtypeStruct((T, RW), BF16),
        scratch_shapes=[pltpu.VMEM((H, RET_HEAD_DIM, RET_HEAD_DIM), F32), pltpu.VMEM((H, C, C), F32)],
        compiler_params=_cparams(1, vmem),
        name="ret",
    )(proj, proj, proj, proj, pos_f, inv2, gn_w)


def _outproj_body(yc_ref, yr_ref, w_ref, x_ref, gpost_ref, gt_ref, gpre_ref, sc_ref, sh_ref, wr_ref,
                  x1_ref, h2b_ref, h2p_ref, lt_ref, wrhi_ref, wrlo_ref):
    CW = yc_ref.shape[1]
    D = x_ref.shape[1]

    @pl.when(pl.program_id(0) == 0)
    def _():
        w = wr_ref[...]
        hi = w.astype(BF16)
        wrhi_ref[...] = hi
        wrlo_ref[...] = (w - hi.astype(F32)).astype(BF16)

    S = D // 2 // LANES
    tm = x_ref.shape[0]
    sub = min(tm, OUTPROJ_SUB)
    for r0 in range(0, tm, sub):
        rows = slice(r0, r0 + sub)
        m = _dot(yc_ref[rows, :], w_ref[0:CW, :]) + _dot(yr_ref[rows, :], w_ref[CW:, :])
        mn = (m * lax.rsqrt(jnp.mean(m * m, axis=-1, keepdims=True) + NORM_EPS)) * gpost_ref[...]
        x1 = x_ref[rows, :] + gt_ref[...] * mn
        x1_ref[rows, :] = x1
        r = x1 * lax.rsqrt(jnp.mean(x1 * x1, axis=-1, keepdims=True) + NORM_EPS)
        h2 = (r * gpre_ref[...]) * (1.0 + sc_ref[...]) + sh_ref[...]
        hb = h2.astype(BF16)
        h2b_ref[rows, :] = hb
        _store_row_tiles(h2p_ref.at[pl.ds(r0 * S, sub * S)], _pack_halves(h2[:, :D // 2], h2[:, D // 2:]))
        hl = (h2 - hb.astype(F32)).astype(BF16)
        lt_ref[:, rows] = (_dot_nt(wrhi_ref[...], hb) + _dot_nt(wrhi_ref[...], hl)) + _dot_nt(wrlo_ref[...], hb)


def _outproj(yc, yr, w_out_bf, x, gpost, gt, gpre, sc, sh, w_router):
    T, D = x.shape
    CW, RW = yc.shape[1], yr.shape[1]
    E = w_router.shape[0]
    S = D // 2 // LANES
    tm = min(256, T)
    vec = pl.BlockSpec((1, D), lambda i: (0, 0))
    row = lambda w: pl.BlockSpec((tm, w), lambda i: (i, 0))
    vmem = (2 * _nbytes((CW + RW, D), BF16) + 2 * _nbytes((E, D), F32) + 2 * _nbytes((E, D), BF16)
            + 4 * _nbytes((tm, D), F32) + 4 * _nbytes((tm, D), BF16) + 2 * _nbytes((tm, D // 2), U32)
            + 2 * _nbytes((E, tm), F32) + 8 * _nbytes((tm, D), F32) + (4 << 20))
    return pl.pallas_call(
        _outproj_body,
        grid=(T // tm,),
        in_specs=[row(CW), row(RW), pl.BlockSpec((CW + RW, D), lambda i: (0, 0)), row(D),
                  vec, vec, vec, vec, vec, pl.BlockSpec((E, D), lambda i: (0, 0))],
        out_specs=[row(D), row(D), pl.BlockSpec((tm * S, LANES), lambda i: (i, 0)),
                   pl.BlockSpec((E, tm), lambda i: (0, i))],
        out_shape=[jax.ShapeDtypeStruct((T, D), F32), jax.ShapeDtypeStruct((T, D), BF16),
                   jax.ShapeDtypeStruct((T * S, LANES), U32), jax.ShapeDtypeStruct((E, T), F32)],
        scratch_shapes=[pltpu.VMEM((E, D), BF16), pltpu.VMEM((E, D), BF16)],
        compiler_params=_cparams(1, vmem),
        name="outproj",
    )(yc, yr, w_out_bf, x, gpost, gt, gpre, sc, sh, w_router)


def _route_body(lt_ref, bias_ref, eidx_ref, gw_ref, rank_ref, cnt_ref, carry_ref):
    E, TL = lt_ref.shape
    G, EG = N_GROUPS, E // N_GROUPS
    NEG, BIG = -jnp.inf, 1e9

    @pl.when(pl.program_id(0) == 0)
    def _():
        carry_ref[...] = jnp.zeros_like(carry_ref)

    scores = jax.nn.sigmoid(lt_ref[...]).reshape(G, EG, TL)
    biased = scores + bias_ref[...].reshape(G, EG, 1)
    eio = lax.broadcasted_iota(I32, (G, EG, TL), 1).astype(F32)
    eid = lax.broadcasted_iota(I32, (G, EG, TL), 0).astype(F32) * EG + eio
    gio = lax.broadcasted_iota(I32, (G, 1, TL), 0).astype(F32)

    m1 = jnp.max(biased, axis=1, keepdims=True)
    i1 = jnp.min(jnp.where(biased == m1, eio, BIG), axis=1, keepdims=True)
    m2 = jnp.max(jnp.where(eio == i1, NEG, biased), axis=1, keepdims=True)
    cur = m1 + m2
    gmask = jnp.zeros((G, 1, TL), F32)
    for _ in range(TOPK_GROUPS):
        mm = jnp.max(cur, axis=0, keepdims=True)
        ii = jnp.min(jnp.where(cur == mm, gio, BIG), axis=0, keepdims=True)
        sel = gio == ii
        gmask = jnp.where(sel, 1.0, gmask)
        cur = jnp.where(sel, NEG, cur)

    cur = jnp.where(gmask > 0.0, biased, NEG)
    sels, idxs, ws = [], [], []
    for _ in range(TOP_K):
        mm = jnp.max(jnp.max(cur, axis=1, keepdims=True), axis=0, keepdims=True)
        cand = jnp.where(cur == mm, eid, BIG)
        ii = jnp.min(jnp.min(cand, axis=1, keepdims=True), axis=0, keepdims=True)
        sel = eid == ii
        w = jnp.where(sel, scores, 0.0)
        ws.append(jnp.sum(jnp.sum(w, axis=1, keepdims=True), axis=0, keepdims=True).reshape(1, TL))
        idxs.append(ii.reshape(1, TL))
        sels.append(sel)
        cur = jnp.where(sel, NEG, cur)

    wsum = ws[0]
    for w in ws[1:]:
        wsum = wsum + w
    for k in range(TOP_K):
        eidx_ref[k:k + 1, :] = idxs[k].astype(I32)
        gw_ref[k:k + 1, :] = ws[k] / wsum * ROUTED_SCALE

    chosen = jnp.zeros((G, EG, TL), F32)
    for sel in sels:
        chosen = jnp.where(sel, 1.0, chosen)
    chosen2 = chosen.reshape(E, TL).astype(BF16)
    before = (lax.broadcasted_iota(I32, (TL, TL), 0) < lax.broadcasted_iota(I32, (TL, TL), 1))
    prefix = _dot(chosen2, jnp.where(before, 1.0, 0.0).astype(BF16))
    total = _dot(chosen2, jnp.ones((TL, TL), BF16))
    carry = carry_ref[...]
    rank_full = (carry + prefix).reshape(G, EG, TL)
    for k in range(TOP_K):
        r = jnp.where(sels[k], rank_full, 0.0)
        r = jnp.sum(jnp.sum(r, axis=1, keepdims=True), axis=0, keepdims=True).reshape(1, TL)
        rank_ref[k:k + 1, :] = r.astype(I32)
    carry = carry + total
    carry_ref[...] = carry
    cnt_ref[...] = carry[:, 0:LANES]


def _route(logits_t, bias):
    E, T = logits_t.shape
    TL = min(256, T)
    tok = pl.BlockSpec((TOP_K, TL), lambda i: (0, i))
    vmem = 64 * _nbytes((E, TL), F32) + (8 << 20)
    return pl.pallas_call(
        _route_body,
        grid=(T // TL,),
        in_specs=[pl.BlockSpec((E, TL), lambda i: (0, i)), pl.BlockSpec((E, 1), lambda i: (0, 0))],
        out_specs=[tok, tok, tok, pl.BlockSpec((E, LANES), lambda i: (0, 0))],
        out_shape=[jax.ShapeDtypeStruct((TOP_K, T), I32), jax.ShapeDtypeStruct((TOP_K, T), F32),
                   jax.ShapeDtypeStruct((TOP_K, T), I32), jax.ShapeDtypeStruct((E, LANES), F32)],
        scratch_shapes=[pltpu.VMEM((E, TL), F32)],
        compiler_params=_cparams(1, vmem),
        name="route",
    )(logits_t, bias.reshape(E, 1))


def _dest_body(eidx_ref, rank_ref, starts_ref, dest_ref):
    E, TL = starts_ref.shape[0], eidx_ref.shape[1]
    eid = lax.broadcasted_iota(I32, (E, TL), 0)
    starts = starts_ref[...]
    for k in range(TOP_K):
        base = jnp.sum(jnp.where(eid == eidx_ref[k:k + 1, :], starts, 0.0), axis=0, keepdims=True)
        dest_ref[k:k + 1, :] = base.astype(I32) + rank_ref[k:k + 1, :]


def _dest(eidx_t, rank_t, starts):
    K, T = eidx_t.shape
    E = starts.shape[0]
    TL = min(512, T)
    tok = pl.BlockSpec((K, TL), lambda i: (0, i))
    return pl.pallas_call(
        _dest_body,
        grid=(T // TL,),
        in_specs=[tok, tok, pl.BlockSpec((E, 1), lambda i: (0, 0))],
        out_specs=tok,
        out_shape=jax.ShapeDtypeStruct((K, T), I32),
        compiler_params=_cparams(1, 16 * _nbytes((E, TL), F32) + (4 << 20)),
        name="dest",
    )(eidx_t, rank_t, starts.astype(F32).reshape(E, 1))


def _row_span(r, S):
    if isinstance(r, int):
        return pl.ds(r * S, S)
    return pl.ds(pl.multiple_of(r * S, S), S)


def _row_copy(src, s, dst, d, sem, S):
    return pltpu.make_async_copy(src.at[_row_span(s, S)], dst.at[_row_span(d, S)], sem)


def _dispatch_body(dest_ref, h_hbm, xs_hbm, hbuf_ref, load_sems, sems, *, S):
    i = pl.program_id(0)
    last = pl.num_programs(0) - 1
    slot = i % 2
    tile_rows = ROW_TILE * S

    def load(tile, dst_slot):
        src = h_hbm.at[pl.ds(pl.multiple_of(tile * tile_rows, tile_rows), tile_rows)]
        return pltpu.make_async_copy(src, hbuf_ref.at[dst_slot], load_sems.at[dst_slot])

    def drain(s):
        for _ in range(ROW_TILE * TOP_K):
            _row_copy(hbuf_ref.at[s], 0, xs_hbm, 0, sems.at[s], S).wait()

    @pl.when(i == 0)
    def _():
        load(0, 0).start()
        pad_rows = MOE_BLK * S
        hbuf_ref[1, pl.ds(0, pad_rows), :] = jnp.zeros((pad_rows, LANES), U32)
        pad = pltpu.make_async_copy(hbuf_ref.at[1, pl.ds(0, pad_rows)],
                                    xs_hbm.at[pl.ds(xs_hbm.shape[0] - pad_rows, pad_rows)], sems.at[1])
        pad.start()
        pad.wait()

    @pl.when(i > 0)
    def _():
        drain(1 - slot)

    @pl.when(i < last)
    def _():
        load(i + 1, 1 - slot).start()

    load(i, slot).wait()
    for t in range(ROW_TILE):
        for k in range(TOP_K):
            _row_copy(hbuf_ref.at[slot], t, xs_hbm, dest_ref[0, 0, t * TOP_K + k], sems.at[slot],
                      S).start(priority=k % 2)

    @pl.when(i == last)
    def _():
        drain(slot)


def _dispatch(dest_tiles, h2p, S):
    T = h2p.shape[0] // S
    n_tiles = T // ROW_TILE
    return pl.pallas_call(
        functools.partial(_dispatch_body, S=S),
        grid=(n_tiles,),
        in_specs=[pl.BlockSpec((1, 1, ROW_TILE * TOP_K), lambda i: (i, 0, 0), memory_space=pltpu.SMEM),
                  pl.BlockSpec(memory_space=pl.ANY)],
        out_specs=pl.BlockSpec(memory_space=pl.ANY),
        out_shape=jax.ShapeDtypeStruct(((T * TOP_K + MOE_BLK) * S, LANES), U32),
        scratch_shapes=[pltpu.VMEM((2, ROW_TILE * S, LANES), U32), pltpu.SemaphoreType.DMA((2,)),
                        pltpu.SemaphoreType.DMA((2,))],
        compiler_params=_cparams(1, 16 << 20),
        name="dispatch",
    )(dest_tiles, h2p)


def _gmm_body(grp_ref, r0_ref, nb_ref, gfirst_ref, gslot_ref, gnext_ref, gnext2_ref, r0n_ref, nbn_ref, nbp_ref,
              xs_hbm, wg_hbm, wu_hbm, wd_hbm, ys_hbm,
              wgf_ref, wuf_ref, wdf_ref, wgb_ref, wub_ref, wdb_ref, xbuf_ref, ybuf_ref, sems, xsems, ysems):
    v = pl.program_id(0)
    last = pl.num_programs(0) - 1
    half = wgb_ref.shape[0] // 2
    S = half // LANES
    sub = MOE_BLK * S
    slot = v % 2
    nb = nb_ref[v]
    landing = ((wg_hbm, wgf_ref, wgb_ref), (wu_hbm, wuf_ref, wub_ref), (wd_hbm, wdf_ref, wdb_ref))

    def sub_rows(r0, b):
        first = (r0 + b * MOE_BLK) * S
        return pl.ds(first if isinstance(r0, int) else pl.multiple_of(first, S), sub)

    def x_copy(r0, b, s):
        return pltpu.make_async_copy(xs_hbm.at[sub_rows(r0, b)], xbuf_ref.at[s, pl.ds(b * sub, sub)], xsems.at[s])

    def y_copy(r0, b, s):
        return pltpu.make_async_copy(ybuf_ref.at[s, pl.ds(b * sub, sub)], ys_hbm.at[sub_rows(r0, b)], ysems.at[s])

    def for_subs(n, fn):
        for b in range(GMM_MAX_SUB):
            @pl.when(b < n)
            def _():
                fn(b)

    def fetch(j, g, slot):
        return pltpu.make_async_copy(landing[j][0].at[g], landing[j][1].at[slot], sems.at[slot, j])

    @pl.when(v == 0)
    def _():
        for j in range(3):
            fetch(j, grp_ref[0], 0).start()

        @pl.when(gnext_ref[0] >= 0)
        def _():
            for j in range(3):
                fetch(j, gnext_ref[0], 1).start()

        for_subs(nb, lambda b: x_copy(r0_ref[0], b, 0).start())

    @pl.when(gfirst_ref[v] == 1)
    def _():
        slot = gslot_ref[v]
        nxt2 = gnext2_ref[v]
        for j in range(3):
            fetch(j, grp_ref[v], slot).wait()
            landing[j][2][...] = landing[j][1][slot].astype(BF16)

            @pl.when(nxt2 >= 0)
            def _():
                fetch(j, nxt2, slot).start()

    for_subs(nb, lambda b: x_copy(r0_ref[v], b, slot).wait())
    for_subs(nbn_ref[v], lambda b: x_copy(r0n_ref[v], b, 1 - slot).start())

    def compute(n_sub):
        M = n_sub * sub
        xl, xh = _unpack_halves(_load_row_tiles(xbuf_ref.at[slot, pl.ds(0, M)], S))
        xl, xh = xl.astype(BF16), xh.astype(BF16)
        gate = _dot(xl, wgb_ref[0:half, :]) + _dot(xh, wgb_ref[half:, :])
        up = _dot(xl, wub_ref[0:half, :]) + _dot(xh, wub_ref[half:, :])
        act = (_silu(gate) * up).astype(BF16)
        y = _dot(act, wdb_ref[...])
        _store_row_tiles(ybuf_ref.at[slot, pl.ds(0, M)], _pack_halves(y[:, :half], y[:, half:]))

    for n_sub in range(1, GMM_MAX_SUB + 1):
        pl.when(nb == n_sub)(functools.partial(compute, n_sub))

    for_subs(nbp_ref[v], lambda b: y_copy(0, b, 1 - slot).wait())
    for_subs(nb, lambda b: y_copy(r0_ref[v], b, slot).start())

    @pl.when(v == last)
    def _():
        for_subs(nb, lambda b: y_copy(0, b, slot).wait())
        ybuf_ref[0, pl.ds(0, sub), :] = jnp.zeros((sub, LANES), U32)
        pad = y_copy(ys_hbm.shape[0] // S - MOE_BLK, 0, 0)
        pad.start()
        pad.wait()


def _gmm(meta, xs, w_gate, w_up, w_down):
    E, D, FF = w_gate.shape
    half = D // 2
    S = half // LANES
    NI = meta[0].shape[0]
    n_meta = len(meta)
    hbm = pl.BlockSpec(memory_space=pl.ANY)
    item_rows = GMM_MAX_SUB * MOE_BLK
    vmem = (6 * _nbytes((D, FF), F32) + 3 * _nbytes((D, FF), BF16) + 4 * _nbytes((item_rows, half), U32)
            + 4 * _nbytes((item_rows, D), F32) + (6 << 20))
    return pl.pallas_call(
        _gmm_body,
        grid_spec=pltpu.PrefetchScalarGridSpec(
            num_scalar_prefetch=n_meta,
            grid=(NI,),
            in_specs=[hbm, hbm, hbm, hbm],
            out_specs=hbm,
            scratch_shapes=[pltpu.VMEM((2, D, FF), F32), pltpu.VMEM((2, D, FF), F32), pltpu.VMEM((2, FF, D), F32),
                            pltpu.VMEM((D, FF), BF16), pltpu.VMEM((D, FF), BF16), pltpu.VMEM((FF, D), BF16),
                            pltpu.VMEM((2, item_rows * S, LANES), U32), pltpu.VMEM((2, item_rows * S, LANES), U32),
                            pltpu.SemaphoreType.DMA((2, 3)), pltpu.SemaphoreType.DMA((2,)),
                            pltpu.SemaphoreType.DMA((2,))]),
        out_shape=jax.ShapeDtypeStruct(xs.shape, U32),
        compiler_params=_cparams(1, vmem),
        name="gmm",
    )(*meta, xs, w_gate, w_up, w_down)


def _gmm_meta(counts, n_rows):
    E = counts.shape[0]
    item_rows = GMM_MAX_SUB * MOE_BLK
    NV = E + n_rows // item_rows
    eids = jnp.arange(E, dtype=I32)
    ends = jnp.cumsum(counts)
    starts = ends - counts
    n_vis = (counts + item_rows - 1) // item_rows
    v_end = jnp.cumsum(n_vis)
    v_start = v_end - n_vis
    total = v_end[-1]
    later = jnp.where((eids[None, :] > eids[:, None]) & (counts[None, :] > 0), eids[None, :], E)
    nxt = jnp.min(later, axis=1)
    nxt2 = jnp.min(jnp.where(eids[None, :] > nxt[:, None], later, E), axis=1)
    nxt = jnp.where(nxt == E, -1, nxt)
    nxt2 = jnp.where(nxt2 == E, -1, nxt2)
    order = jnp.cumsum((counts > 0).astype(I32)) - 1
    v = jnp.arange(NV, dtype=I32)
    vc = jnp.minimum(v, total - 1)
    grp = jnp.sum((v_end[None, :] <= vc[:, None]).astype(I32), axis=1)
    onehot = grp[:, None] == eids[None, :]
    pick = lambda a: jnp.sum(jnp.where(onehot, a[None, :], 0), axis=1)
    valid = v < total
    part = vc - pick(v_start)
    r0 = pick(starts) + part * item_rows
    n_sub = jnp.where(valid, (jnp.minimum(pick(ends) - r0, item_rows) + MOE_BLK - 1) // MOE_BLK, 0)
    gfirst = (part == 0) & valid
    nxt_item = lambda a: jnp.concatenate([a[1:], jnp.zeros((1,), a.dtype)])
    prv_item = lambda a: jnp.concatenate([jnp.zeros((1,), a.dtype), a[:-1]])
    meta = (grp, r0, n_sub, gfirst, pick(order) % 2, pick(nxt), pick(nxt2), nxt_item(r0), nxt_item(n_sub),
            prv_item(n_sub))
    return tuple(m.astype(I32) for m in meta), starts


def _shared_body(h_ref, wg_ref, wu_ref, wd_ref, o_ref):
    h = h_ref[...]
    act = (_silu(_dot(h, wg_ref[...])) * _dot(h, wu_ref[...])).astype(BF16)
    o_ref[...] = _dot(act, wd_ref[...]).astype(o_ref.dtype)


def _shared(h2b, wg_bf, wu_bf, wd_bf):
    T, D = h2b.shape
    SF = wg_bf.shape[1]
    tm = min(512, T)
    full = lambda a, b: pl.BlockSpec((a, b), lambda i: (0, 0))
    vmem = 6 * _nbytes((D, SF), BF16) + 4 * _nbytes((tm, D), BF16) + 3 * _nbytes((tm, D), F32) + (4 << 20)
    return pl.pallas_call(
        _shared_body,
        grid=(T // tm,),
        in_specs=[pl.BlockSpec((tm, D), lambda i: (i, 0)), full(D, SF), full(D, SF), full(SF, D)],
        out_specs=pl.BlockSpec((tm, D), lambda i: (i, 0)),
        out_shape=jax.ShapeDtypeStruct((T, D), BF16),
        compiler_params=_cparams(1, vmem),
        name="shared",
    )(h2b, wg_bf, wu_bf, wd_bf)


def _combine_body(dest0_ref, destn_ref, ys_hbm, gw_ref, sh_ref, x1_ref, gt_ref, gpost_ref, o_ref, ybuf_ref, sems):
    i = pl.program_id(0)
    last = pl.num_programs(0) - 1
    D = x1_ref.shape[1]
    half = D // 2
    S = half // LANES

    def gather(dest_ref, slot):
        for t in range(ROW_TILE):
            for k in range(TOP_K):
                _row_copy(ys_hbm, dest_ref[0, 0, t * TOP_K + k], ybuf_ref.at[slot, k], t,
                          sems.at[slot], S).start(priority=k % 2)

    def drain(slot):
        for _ in range(ROW_TILE * TOP_K):
            _row_copy(ys_hbm, 0, ybuf_ref.at[slot, 0], 0, sems.at[slot], S).wait()

    @pl.when(i == 0)
    def _():
        gather(dest0_ref, 0)

    gather(destn_ref, (i + 1) % 2)
    drain(i % 2)
    slot = i % 2

    wb = [jnp.broadcast_to(gw_ref[:, k:k + 1], (ROW_TILE, LANES)) for k in range(TOP_K)]
    ssq = jnp.zeros((ROW_TILE, 1), F32)
    for s in range(S):
        c_lo = slice(s * LANES, (s + 1) * LANES)
        c_hi = slice(half + s * LANES, half + (s + 1) * LANES)
        fl = sh_ref[:, c_lo].astype(F32)
        fh = sh_ref[:, c_hi].astype(F32)
        for k in range(TOP_K):
            yl, yh = _unpack_halves(_load_row_chunk(ybuf_ref.at[slot, k], s, S))
            fl = fl + yl * wb[k]
            fh = fh + yh * wb[k]
        ssq = ssq + jnp.sum(fl * fl, axis=-1, keepdims=True) + jnp.sum(fh * fh, axis=-1, keepdims=True)
        o_ref[:, c_lo] = fl
        o_ref[:, c_hi] = fh
    r = lax.rsqrt(ssq / D + NORM_EPS)
    o_ref[...] = x1_ref[...] + gt_ref[...] * ((o_ref[...] * r) * gpost_ref[...])

    @pl.when(i == last)
    def _():
        drain((i + 1) % 2)


def _combine(dest_tiles, ys, gw, shared, x1, gt, gpost):
    T, D = x1.shape
    half = D // 2
    n_tiles = T // ROW_TILE
    row = lambda w: pl.BlockSpec((ROW_TILE, w), lambda i: (i, 0))
    vec = pl.BlockSpec((1, D), lambda i: (0, 0))
    idx = lambda f: pl.BlockSpec((1, 1, ROW_TILE * TOP_K), f, memory_space=pltpu.SMEM)
    S = half // LANES
    vmem = (2 * _nbytes((TOP_K, ROW_TILE, half), U32) + 4 * _nbytes((ROW_TILE, D), F32)
            + 2 * _nbytes((ROW_TILE, D), BF16) + 8 * _nbytes((ROW_TILE, D), F32) + (4 << 20))
    return pl.pallas_call(
        _combine_body,
        grid=(n_tiles,),
        in_specs=[idx(lambda i: (0, 0, 0)), idx(lambda i: (jnp.minimum(i + 1, n_tiles - 1), 0, 0)),
                  pl.BlockSpec(memory_space=pl.ANY), row(TOP_K), row(D), row(D), vec, vec],
        out_specs=row(D),
        out_shape=jax.ShapeDtypeStruct((T, D), F32),
        scratch_shapes=[pltpu.VMEM((2, TOP_K, ROW_TILE * S, LANES), U32), pltpu.SemaphoreType.DMA((2,))],
        compiler_params=_cparams(1, vmem),
        name="combine",
    )(dest_tiles, dest_tiles, ys, gw, shared, x1, gt, gpost)


def _layer(x, c, pos_f, w_ada, b_ada, g_mix_pre, g_mix_post, g_ffn_pre, g_ffn_post, w_in, conv_w, ret_gn_w,
           w_out, w_router, router_bias, w_gate, w_up, w_down, ws_gate, ws_up, ws_down):
    T, D = x.shape
    CW = conv_w.shape[-1]
    RW = ret_gn_w.shape[-1]
    assert CW == RW and w_in.shape[1] == 3 * CW + 4 * RW and w_out.shape[0] == CW + RW
    assert T % ROW_TILE == 0 and (T * TOP_K) % (GMM_MAX_SUB * MOE_BLK) == 0 and D % (2 * LANES) == 0
    assert MOE_BLK <= ROW_TILE
    vec = lambda a: a.reshape(1, D)

    mod = _ada(c, w_ada, b_ada)
    sh1, sc1, gt1, sh2, sc2, gt2 = (mod[:, i * D:(i + 1) * D] for i in range(6))

    proj = _inproj(x, vec(g_mix_pre), sc1, sh1, w_in.astype(BF16), tn=CW)
    y_conv = _conv(proj, conv_w, CW)
    y_ret = _retention(proj, pos_f, ret_gn_w.reshape(1, RW), RW, col0_blocks=3)
    x1, h2b, h2p, logits_t = _outproj(y_conv, y_ret, w_out.astype(BF16), x, vec(g_mix_post), gt1,
                                      vec(g_ffn_pre), sc2, sh2, w_router)

    eidx_t, gw_t, rank_t, cnt = _route(logits_t, router_bias)
    counts = cnt[:, 0].astype(I32)
    meta, starts = _gmm_meta(counts, T * TOP_K)
    dest = _dest(eidx_t, rank_t, starts).T
    dest_tiles = dest.reshape(T // ROW_TILE, 1, ROW_TILE * TOP_K)

    xs = _dispatch(dest_tiles, h2p, D // 2 // LANES)
    ys = _gmm(meta, xs, w_gate, w_up, w_down)
    shared = _shared(h2b, ws_gate.astype(BF16), ws_up.astype(BF16), ws_down.astype(BF16))
    return _combine(dest_tiles, ys, gw_t.T, shared, x1, gt2, vec(g_ffn_post))


def kernel(x, c, positions, w_ada, b_ada, g_mix_pre, g_mix_post, g_ffn_pre, g_ffn_post, w_in, conv_w, ret_gn_w,
           w_out, w_router, router_bias, w_gate, w_up, w_down, ws_gate, ws_up, ws_down):
    B, S, D = x.shape
    assert B == 1, "one sequence per call"
    xt = x.reshape(S, D)
    pos_f = positions.astype(F32).reshape(S, 1)
    for l in range(w_ada.shape[0]):
        xt = _layer(xt, c, pos_f, w_ada[l], b_ada[l], g_mix_pre[l], g_mix_post[l], g_ffn_pre[l], g_ffn_post[l],
                    w_in[l], conv_w[l], ret_gn_w[l], w_out[l], w_router[l], router_bias[l], w_gate[l], w_up[l],
                    w_down[l], ws_gate[l], ws_up[l], ws_down[l])
    return xt.reshape(B, S, D)
```

```python
import functools
import math

import jax
import jax.numpy as jnp
from jax import lax
from jax.experimental import pallas as pl
from jax.experimental.pallas import tpu as pltpu

F32 = jnp.float32
BF16 = jnp.bfloat16
U32 = jnp.uint32
I32 = jnp.int32

CONV_K = 3
RET_HEAD_DIM = 128
ROPE_THETA = 10000.0
TOP_K = 8
N_GROUPS = 8
TOPK_GROUPS = 4
ROUTED_SCALE = 2.5
NORM_EPS = 1e-6
GN_EPS = 1e-5

LANES = 128
SUBLANES = 8
VMEM_LIMIT_CAP = 56 * 1024 * 1024

RET_CHUNK = 256
MOE_BLK = 128
GMM_MAX_SUB = 4
ROW_TILE = 128
OUTPROJ_SUB = 256
COMBINE_ROWS = 32
INPROJ_PARTS = 4


def _cparams(n_axes, vmem_bytes):
    return pltpu.CompilerParams(
        dimension_semantics=("arbitrary",) * n_axes,
        vmem_limit_bytes=int(min(max(vmem_bytes, 16 * 1024 * 1024), VMEM_LIMIT_CAP)))


def _nbytes(shape, dtype):
    return math.prod(shape) * jnp.dtype(dtype).itemsize


def _dot(a, b):
    return jnp.dot(a, b, preferred_element_type=F32)


def _dot_nt(a, b):
    return lax.dot_general(a, b, (((1,), (1,)), ((), ())), preferred_element_type=F32)


def _dot_tn(a, b):
    return lax.dot_general(a, b, (((0,), (0,)), ((), ())), preferred_element_type=F32)


def _silu(v):
    return v * jax.nn.sigmoid(v)


def _pack_halves(lo, hi):
    lo_b = lax.bitcast_convert_type(lo.astype(BF16).astype(F32), U32)
    hi_b = lax.bitcast_convert_type(hi.astype(BF16).astype(F32), U32)
    return (lo_b >> 16) | (hi_b & jnp.uint32(0xFFFF0000))


def _unpack_halves(w):
    lo = lax.bitcast_convert_type(w << 16, F32)
    hi = lax.bitcast_convert_type(w & jnp.uint32(0xFFFF0000), F32)
    return lo, hi


def _store_row_tiles(ref, val):
    rows, S = val.shape[0], val.shape[1] // LANES
    for s in range(S):
        ref[pl.ds(s, rows, stride=S), :] = val[:, s * LANES:(s + 1) * LANES]


def _load_row_chunk(ref, s, S):
    return ref[pl.ds(s, ref.shape[0] // S, stride=S), :]


def _load_row_tiles(ref, S):
    return jnp.concatenate([_load_row_chunk(ref, s, S) for s in range(S)], axis=1)


def _ada_body(cb_ref, w_ref, b_ref, o_ref):
    s = _silu(cb_ref[...])
    for j in range(o_ref.shape[-1] // LANES):
        sl = slice(j * LANES, (j + 1) * LANES)
        o_ref[:, sl] = jnp.sum(w_ref[:, sl] * s, axis=0, keepdims=True) + b_ref[:, sl]


def _ada(c, w_ada, b_ada):
    D, N = w_ada.shape
    tn = min(1024, N)
    cb = jnp.broadcast_to(c.reshape(D, 1), (D, LANES))
    vmem = 2 * _nbytes((D, tn), F32) + 3 * _nbytes((D, LANES), F32) + (4 << 20)
    return pl.pallas_call(
        _ada_body,
        grid=(N // tn,),
        in_specs=[pl.BlockSpec((D, LANES), lambda j: (0, 0)),
                  pl.BlockSpec((D, tn), lambda j: (0, j)),
                  pl.BlockSpec((1, tn), lambda j: (0, j))],
        out_specs=pl.BlockSpec((1, tn), lambda j: (0, j)),
        out_shape=jax.ShapeDtypeStruct((1, N), F32),
        compiler_params=_cparams(1, vmem),
        name="ada",
    )(cb, w_ada, b_ada.reshape(1, N))


def _inproj_body(x_ref, g_ref, sc_ref, sh_ref, w_ref, o_ref, h_ref):
    tm = x_ref.shape[0]
    part = tm // INPROJ_PARTS

    @pl.when(pl.program_id(1) == 0)
    def _():
        for p in range(INPROJ_PARTS):
            rows = slice(p * part, (p + 1) * part)
            xf = x_ref[rows, :]
            r = xf * lax.rsqrt(jnp.mean(xf * xf, axis=-1, keepdims=True) + NORM_EPS)
            h = ((r * g_ref[...]) * (1.0 + sc_ref[...]) + sh_ref[...]).astype(BF16)
            h_ref[rows, :] = h
            o_ref[rows, :] = _dot(h, w_ref[...]).astype(o_ref.dtype)

    @pl.when(pl.program_id(1) != 0)
    def _():
        o_ref[...] = _dot(h_ref[...], w_ref[...]).astype(o_ref.dtype)


def _inproj(x, g, sc, sh, w_bf, tn):
    T, D = x.shape
    N = w_bf.shape[1]
    tm = min(1024, T)
    vec = pl.BlockSpec((1, D), lambda i, j: (0, 0))
    vmem = (2 * _nbytes((tm, D), F32) + _nbytes((tm, D), BF16) + 2 * _nbytes((D, tn), BF16)
            + 2 * _nbytes((tm, tn), BF16) + 2 * _nbytes((tm, tn), F32) + 2 * _nbytes((tm, D), F32) + (4 << 20))
    return pl.pallas_call(
        _inproj_body,
        grid=(T // tm, N // tn),
        in_specs=[pl.BlockSpec((tm, D), lambda i, j: (i, 0)), vec, vec, vec,
                  pl.BlockSpec((D, tn), lambda i, j: (0, j))],
        out_specs=pl.BlockSpec((tm, tn), lambda i, j: (i, j)),
        out_shape=jax.ShapeDtypeStruct((T, N), BF16),
        scratch_shapes=[pltpu.VMEM((tm, D), BF16)],
        compiler_params=_cparams(2, vmem),
        name="inproj",
    )(x, g, sc, sh, w_bf)


def _conv_body(cb_ref, cc_ref, cx_ref, w_ref, o_ref, u_ref):
    tc = cb_ref.shape[0]

    @pl.when(pl.program_id(1) == 0)
    def _():
        u_ref[0:SUBLANES, :] = jnp.zeros((SUBLANES, u_ref.shape[1]), F32)

    u = cc_ref[...].astype(F32) * cx_ref[...].astype(F32)
    u_ref[SUBLANES:SUBLANES + tc, :] = u
    u1 = u_ref[SUBLANES - 1:SUBLANES - 1 + tc, :]
    u2 = u_ref[SUBLANES - 2:SUBLANES - 2 + tc, :]
    conv = u2 * w_ref[0:1, :] + u1 * w_ref[1:2, :] + u * w_ref[2:3, :]
    o_ref[...] = (cb_ref[...].astype(F32) * conv).astype(o_ref.dtype)
    u_ref[0:SUBLANES, :] = u_ref[tc:tc + SUBLANES, :]


def _conv(proj, conv_w, CW):
    T = proj.shape[0]
    tc = min(512, T)
    tw = min(512, CW)
    nb = CW // tw
    vmem = 8 * _nbytes((tc, tw), BF16) + 8 * _nbytes((tc + SUBLANES, tw), F32) + (4 << 20)
    return pl.pallas_call(
        _conv_body,
        grid=(nb, T // tc),
        in_specs=[pl.BlockSpec((tc, tw), lambda j, i: (i, j)),
                  pl.BlockSpec((tc, tw), lambda j, i: (i, nb + j)),
                  pl.BlockSpec((tc, tw), lambda j, i: (i, 2 * nb + j)),
                  pl.BlockSpec((CONV_K, tw), lambda j, i: (0, j))],
        out_specs=pl.BlockSpec((tc, tw), lambda j, i: (i, j)),
        out_shape=jax.ShapeDtypeStruct((T, CW), BF16),
        scratch_shapes=[pltpu.VMEM((tc + SUBLANES, tw), F32)],
        compiler_params=_cparams(2, vmem),
        name="conv",
    )(proj, proj, proj, conv_w)


def _ret_body(q_ref, k_ref, v_ref, g_ref, pos_ref, inv_ref, gnw_ref, o_ref, state_ref, dec_ref, *, H, C):
    dh = RET_HEAD_DIM
    log_gamma = [math.log1p(-(2.0 ** (-5.0 - h))) for h in range(H)]

    @pl.when(pl.program_id(0) == 0)
    def _():
        state_ref[...] = jnp.zeros_like(state_ref)
        rel = (lax.broadcasted_iota(I32, (C, C), 0) - lax.broadcasted_iota(I32, (C, C), 1)).astype(F32)
        for h in range(H):
            dec_ref[h] = jnp.where(rel >= 0.0, jnp.exp(log_gamma[h] * jnp.maximum(rel, 0.0)), 0.0)

    ang = pos_ref[...] * inv_ref[...]
    cos2 = jnp.cos(ang)
    sin2 = jnp.where(lax.broadcasted_iota(I32, (C, dh), 1) < dh // 2, -jnp.sin(ang), jnp.sin(ang))
    ridx = lax.broadcasted_iota(I32, (C, 1), 0).astype(F32)

    def rope(t):
        return t * cos2 + pltpu.roll(t, dh // 2, 1) * sin2

    for h in range(H):
        sl = slice(h * dh, (h + 1) * dh)
        lg = log_gamma[h]
        qr = rope(q_ref[:, sl].astype(F32))
        kr = rope(k_ref[:, sl].astype(F32)) * (dh ** -0.5)
        v = v_ref[:, sl]
        q_decay = jnp.exp(lg * (ridx + 1.0))
        k_decay = jnp.exp(lg * (C - 1.0 - ridx))
        scores = _dot_nt(qr.astype(BF16), kr.astype(BF16)) * dec_ref[h]
        state = state_ref[h]
        o = _dot(scores.astype(BF16), v) + _dot((qr * q_decay).astype(BF16), state.astype(BF16))
        state_ref[h] = state * math.exp(lg * C) + _dot_tn((kr * k_decay).astype(BF16), v)
        mu = jnp.mean(o, axis=-1, keepdims=True)
        d = o - mu
        var = jnp.mean(d * d, axis=-1, keepdims=True)
        on = d * lax.rsqrt(var + GN_EPS) * gnw_ref[:, sl]
        o_ref[:, sl] = (_silu(g_ref[:, sl].astype(F32)) * on).astype(o_ref.dtype)


def _retention(proj, pos_f, gn_w, RW, col0_blocks):
    T = proj.shape[0]
    C = min(RET_CHUNK, T)
    H = RW // RET_HEAD_DIM
    half = RET_HEAD_DIM // 2
    inv = ROPE_THETA ** (-jnp.arange(half, dtype=F32) / half)
    inv2 = jnp.concatenate([inv, inv]).reshape(1, RET_HEAD_DIM)
    blk = lambda off: pl.BlockSpec((C, RW), lambda i: (i, col0_blocks + off))
    vmem = (10 * _nbytes((C, RW), BF16) + _nbytes((H, RET_HEAD_DIM, RET_HEAD_DIM), F32)
            + _nbytes((H, C, C), F32) + 24 * _nbytes((C, C), F32) + (8 << 20))
    return pl.pallas_call(
        functools.partial(_ret_body, H=H, C=C),
        grid=(T // C,),
        in_specs=[blk(0), blk(1), blk(2), blk(3),
                  pl.BlockSpec((C, 1), lambda i: (i, 0)),
                  pl.BlockSpec((1, RET_HEAD_DIM), lambda i: (0, 0)),
                  pl.BlockSpec((1, RW), lambda i: (0, 0))],
        out_specs=pl.BlockSpec((C, RW), lambda i: (i, 0)),
        out_shape=jax.ShapeDtypeStruct((T, RW), BF16),
        scratch_shapes=[pltpu.VMEM((H, RET_HEAD_DIM, RET_HEAD_DIM), F32), pltpu.VMEM((H, C, C), F32)],
        compiler_params=_cparams(1, vmem),
        name="ret",
    )(proj, proj, proj, proj, pos_f, inv2, gn_w)


def _outproj_body(yc_ref, yr_ref, w_ref, x_ref, gpost_ref, gt_ref, gpre_ref, sc_ref, sh_ref, wr_ref,
                  x1_ref, h2b_ref, h2p_ref, lt_ref, wrhi_ref, wrlo_ref):
    CW = yc_ref.shape[1]
    D = x_ref.shape[1]

    @pl.when(pl.program_id(0) == 0)
    def _():
        w = wr_ref[...]
        hi = w.astype(BF16)
        wrhi_ref[...] = hi
        wrlo_ref[...] = (w - hi.astype(F32)).astype(BF16)

    S = D // 2 // LANES
    tm = x_ref.shape[0]
    sub = min(tm, OUTPROJ_SUB)
    for r0 in range(0, tm, sub):
        rows = slice(r0, r0 + sub)
        m = _dot(yc_ref[rows, :], w_ref[0:CW, :]) + _dot(yr_ref[rows, :], w_ref[CW:, :])
        mn = (m * lax.rsqrt(jnp.mean(m * m, axis=-1, keepdims=True) + NORM_EPS)) * gpost_ref[...]
        x1 = x_ref[rows, :] + gt_ref[...] * mn
        x1_ref[rows, :] = x1
        r = x1 * lax.rsqrt(jnp.mean(x1 * x1, axis=-1, keepdims=True) + NORM_EPS)
        h2 = (r * gpre_ref[...]) * (1.0 + sc_ref[...]) + sh_ref[...]
        hb = h2.astype(BF16)
        h2b_ref[rows, :] = hb
        _store_row_tiles(h2p_ref.at[pl.ds(r0 * S, sub * S)], _pack_halves(h2[:, :D // 2], h2[:, D // 2:]))
        hl = (h2 - hb.astype(F32)).astype(BF16)
        lt_ref[:, rows] = (_dot_nt(wrhi_ref[...], hb) + _dot_nt(wrhi_ref[...], hl)) + _dot_nt(wrlo_ref[...], hb)


def _outproj(yc, yr, w_out_bf, x, gpost, gt, gpre, sc, sh, w_router):
    T, D = x.shape
    CW, RW = yc.shape[1], yr.shape[1]
    E = w_router.shape[0]
    S = D // 2 // LANES
    tm = min(2 * OUTPROJ_SUB, T)
    vec = pl.BlockSpec((1, D), lambda i: (0, 0))
    row = lambda w: pl.BlockSpec((tm, w), lambda i: (i, 0))
    const = lambda a, b: pl.BlockSpec((a, b), lambda i: (0, 0), pipeline_mode=pl.Buffered(1))
    vmem = (_nbytes((CW + RW, D), BF16) + _nbytes((E, D), F32) + 2 * _nbytes((E, D), BF16)
            + 4 * _nbytes((tm, D), F32) + 4 * _nbytes((tm, D), BF16) + 2 * _nbytes((tm, D // 2), U32)
            + 2 * _nbytes((E, tm), F32) + 5 * _nbytes((tm, D), F32) + (4 << 20))
    return pl.pallas_call(
        _outproj_body,
        grid=(T // tm,),
        in_specs=[row(CW), row(RW), const(CW + RW, D), row(D),
                  vec, vec, vec, vec, vec, const(E, D)],
        out_specs=[row(D), row(D), pl.BlockSpec((tm * S, LANES), lambda i: (i, 0)),
                   pl.BlockSpec((E, tm), lambda i: (0, i))],
        out_shape=[jax.ShapeDtypeStruct((T, D), F32), jax.ShapeDtypeStruct((T, D), BF16),
                   jax.ShapeDtypeStruct((T * S, LANES), U32), jax.ShapeDtypeStruct((E, T), F32)],
        scratch_shapes=[pltpu.VMEM((E, D), BF16), pltpu.VMEM((E, D), BF16)],
        compiler_params=_cparams(1, vmem),
        name="outproj",
    )(yc, yr, w_out_bf, x, gpost, gt, gpre, sc, sh, w_router)


def _route_body(lt_ref, bias_ref, eidx_ref, gw_ref, rank_ref, cnt_ref, carry_ref):
    E, TL = lt_ref.shape
    G, EG = N_GROUPS, E // N_GROUPS
    NEG, BIG = -jnp.inf, 1e9

    @pl.when(pl.program_id(0) == 0)
    def _():
        carry_ref[...] = jnp.zeros_like(carry_ref)

    scores = jax.nn.sigmoid(lt_ref[...]).reshape(G, EG, TL)
    biased = scores + bias_ref[...].reshape(G, EG, 1)
    eio = lax.broadcasted_iota(I32, (G, EG, TL), 1).astype(F32)
    eid = lax.broadcasted_iota(I32, (G, EG, TL), 0).astype(F32) * EG + eio
    gio = lax.broadcasted_iota(I32, (G, 1, TL), 0).astype(F32)

    m1 = jnp.max(biased, axis=1, keepdims=True)
    i1 = jnp.min(jnp.where(biased == m1, eio, BIG), axis=1, keepdims=True)
    m2 = jnp.max(jnp.where(eio == i1, NEG, biased), axis=1, keepdims=True)
    cur = m1 + m2
    gmask = jnp.zeros((G, 1, TL), F32)
    for _ in range(TOPK_GROUPS):
        mm = jnp.max(cur, axis=0, keepdims=True)
        ii = jnp.min(jnp.where(cur == mm, gio, BIG), axis=0, keepdims=True)
        sel = gio == ii
        gmask = jnp.where(sel, 1.0, gmask)
        cur = jnp.where(sel, NEG, cur)

    cur = jnp.where(gmask > 0.0, biased, NEG)
    sels, idxs, ws = [], [], []
    for _ in range(TOP_K):
        mm = jnp.max(jnp.max(cur, axis=1, keepdims=True), axis=0, keepdims=True)
        cand = jnp.where(cur == mm, eid, BIG)
        ii = jnp.min(jnp.min(cand, axis=1, keepdims=True), axis=0, keepdims=True)
        sel = eid == ii
        w = jnp.where(sel, scores, 0.0)
        ws.append(jnp.sum(jnp.sum(w, axis=1, keepdims=True), axis=0, keepdims=True).reshape(1, TL))
        idxs.append(ii.reshape(1, TL))
        sels.append(sel)
        cur = jnp.where(sel, NEG, cur)

    wsum = ws[0]
    for w in ws[1:]:
        wsum = wsum + w
    for k in range(TOP_K):
        eidx_ref[k:k + 1, :] = idxs[k].astype(I32)
        gw_ref[k:k + 1, :] = ws[k] / wsum * ROUTED_SCALE

    chosen = jnp.zeros((G, EG, TL), F32)
    for sel in sels:
        chosen = jnp.where(sel, 1.0, chosen)
    chosen2 = chosen.reshape(E, TL).astype(BF16)
    before = (lax.broadcasted_iota(I32, (TL, TL), 0) < lax.broadcasted_iota(I32, (TL, TL), 1))
    prefix = _dot(chosen2, jnp.where(before, 1.0, 0.0).astype(BF16))
    total = _dot(chosen2, jnp.ones((TL, TL), BF16))
    carry = carry_ref[...]
    rank_full = (carry + prefix).reshape(G, EG, TL)
    for k in range(TOP_K):
        r = jnp.where(sels[k], rank_full, 0.0)
        r = jnp.sum(jnp.sum(r, axis=1, keepdims=True), axis=0, keepdims=True).reshape(1, TL)
        rank_ref[k:k + 1, :] = r.astype(I32)
    carry = carry + total
    carry_ref[...] = carry
    cnt_ref[...] = carry[:, 0:LANES]


def _route(logits_t, bias):
    E, T = logits_t.shape
    TL = min(256, T)
    tok = pl.BlockSpec((TOP_K, TL), lambda i: (0, i))
    vmem = 64 * _nbytes((E, TL), F32) + (8 << 20)
    return pl.pallas_call(
        _route_body,
        grid=(T // TL,),
        in_specs=[pl.BlockSpec((E, TL), lambda i: (0, i)), pl.BlockSpec((E, 1), lambda i: (0, 0))],
        out_specs=[tok, tok, tok, pl.BlockSpec((E, LANES), lambda i: (0, 0))],
        out_shape=[jax.ShapeDtypeStruct((TOP_K, T), I32), jax.ShapeDtypeStruct((TOP_K, T), F32),
                   jax.ShapeDtypeStruct((TOP_K, T), I32), jax.ShapeDtypeStruct((E, LANES), F32)],
        scratch_shapes=[pltpu.VMEM((E, TL), F32)],
        compiler_params=_cparams(1, vmem),
        name="route",
    )(logits_t, bias.reshape(E, 1))


def _dest_body(eidx_ref, rank_ref, starts_ref, dest_ref, *, S):
    E, TL = starts_ref.shape[0], eidx_ref.shape[1]
    eid = lax.broadcasted_iota(I32, (E, TL), 0)
    starts = starts_ref[...]
    for k in range(TOP_K):
        base = jnp.sum(jnp.where(eid == eidx_ref[k:k + 1, :], starts, 0.0), axis=0, keepdims=True)
        dest_ref[k:k + 1, :] = (base.astype(I32) + rank_ref[k:k + 1, :]) * S


def _dest(eidx_t, rank_t, starts, S):
    K, T = eidx_t.shape
    E = starts.shape[0]
    TL = min(512, T)
    tok = pl.BlockSpec((K, TL), lambda i: (0, i))
    return pl.pallas_call(
        functools.partial(_dest_body, S=S),
        grid=(T // TL,),
        in_specs=[tok, tok, pl.BlockSpec((E, 1), lambda i: (0, 0))],
        out_specs=tok,
        out_shape=jax.ShapeDtypeStruct((K, T), I32),
        compiler_params=_cparams(1, 16 * _nbytes((E, TL), F32) + (4 << 20)),
        name="dest",
    )(eidx_t, rank_t, starts.astype(F32).reshape(E, 1))


def _row_span(r, S):
    if isinstance(r, int):
        return pl.ds(r * S, S)
    return pl.ds(pl.multiple_of(r, S), S)


def _row_copy(src, s, dst, d, sem, S):
    return pltpu.make_async_copy(src.at[_row_span(s, S)], dst.at[_row_span(d, S)], sem)


def _dispatch_body(dest_ref, h_hbm, xs_hbm, hbuf_ref, load_sems, sems, *, S):
    i = pl.program_id(0)
    last = pl.num_programs(0) - 1
    slot = i % 2
    tile_rows = ROW_TILE * S

    def load(tile, dst_slot):
        src = h_hbm.at[pl.ds(pl.multiple_of(tile * tile_rows, tile_rows), tile_rows)]
        return pltpu.make_async_copy(src, hbuf_ref.at[dst_slot], load_sems.at[dst_slot])

    def drain(s):
        for _ in range(ROW_TILE * TOP_K):
            _row_copy(hbuf_ref.at[s], 0, xs_hbm, 0, sems.at[s], S).wait()

    @pl.when(i == 0)
    def _():
        load(0, 0).start()
        pad_rows = MOE_BLK * S
        hbuf_ref[1, pl.ds(0, pad_rows), :] = jnp.zeros((pad_rows, LANES), U32)
        pad = pltpu.make_async_copy(hbuf_ref.at[1, pl.ds(0, pad_rows)],
                                    xs_hbm.at[pl.ds(xs_hbm.shape[0] - pad_rows, pad_rows)], sems.at[1])
        pad.start()
        pad.wait()

    @pl.when(i > 0)
    def _():
        drain(1 - slot)

    @pl.when(i < last)
    def _():
        load(i + 1, 1 - slot).start()

    load(i, slot).wait()
    for t in range(ROW_TILE):
        for k in range(TOP_K):
            _row_copy(hbuf_ref.at[slot], t, xs_hbm, dest_ref[0, 0, t * TOP_K + k], sems.at[slot],
                      S).start(priority=k % 2)

    @pl.when(i == last)
    def _():
        drain(slot)


def _dispatch(dest_tiles, h2p, S):
    T = h2p.shape[0] // S
    n_tiles = T // ROW_TILE
    return pl.pallas_call(
        functools.partial(_dispatch_body, S=S),
        grid=(n_tiles,),
        in_specs=[pl.BlockSpec((1, 1, ROW_TILE * TOP_K), lambda i: (i, 0, 0), memory_space=pltpu.SMEM),
                  pl.BlockSpec(memory_space=pl.ANY)],
        out_specs=pl.BlockSpec(memory_space=pl.ANY),
        out_shape=jax.ShapeDtypeStruct(((T * TOP_K + MOE_BLK) * S, LANES), U32),
        scratch_shapes=[pltpu.VMEM((2, ROW_TILE * S, LANES), U32), pltpu.SemaphoreType.DMA((2,)),
                        pltpu.SemaphoreType.DMA((2,))],
        compiler_params=_cparams(1, 16 << 20),
        name="dispatch",
    )(dest_tiles, h2p)


def _gmm_body(grp_ref, r0_ref, nb_ref, gfirst_ref, gslot_ref, gnext_ref, gnext2_ref, r0n_ref, nbn_ref, nbp_ref,
              xs_hbm, wg_hbm, wu_hbm, wd_hbm, ys_hbm,
              wgf_ref, wuf_ref, wdf_ref, wgb_ref, wub_ref, wdb_ref, xbuf_ref, ybuf_ref, sems, xsems, ysems):
    v = pl.program_id(0)
    last = pl.num_programs(0) - 1
    half = wgb_ref.shape[0] // 2
    S = half // LANES
    sub = MOE_BLK * S
    slot = v % 2
    nb = nb_ref[v]
    landing = ((wg_hbm, wgf_ref, wgb_ref), (wu_hbm, wuf_ref, wub_ref), (wd_hbm, wdf_ref, wdb_ref))

    def sub_rows(r0, b):
        first = (r0 + b * MOE_BLK) * S
        return pl.ds(first if isinstance(r0, int) else pl.multiple_of(first, S), sub)

    def x_copy(r0, b, s):
        return pltpu.make_async_copy(xs_hbm.at[sub_rows(r0, b)], xbuf_ref.at[s, pl.ds(b * sub, sub)], xsems.at[s])

    def y_copy(r0, b, s):
        return pltpu.make_async_copy(ybuf_ref.at[s, pl.ds(b * sub, sub)], ys_hbm.at[sub_rows(r0, b)], ysems.at[s])

    def for_subs(n, fn):
        for b in range(GMM_MAX_SUB):
            @pl.when(b < n)
            def _():
                fn(b)

    def fetch(j, g, slot):
        return pltpu.make_async_copy(landing[j][0].at[g], landing[j][1].at[slot], sems.at[slot, j])

    @pl.when(v == 0)
    def _():
        for j in range(3):
            fetch(j, grp_ref[0], 0).start()

        @pl.when(gnext_ref[0] >= 0)
        def _():
            for j in range(3):
                fetch(j, gnext_ref[0], 1).start()

        for_subs(nb, lambda b: x_copy(r0_ref[0], b, 0).start())

    @pl.when(gfirst_ref[v] == 1)
    def _():
        slot = gslot_ref[v]
        nxt2 = gnext2_ref[v]
        for j in range(3):
            fetch(j, grp_ref[v], slot).wait()
            landing[j][2][...] = landing[j][1][slot].astype(BF16)

            @pl.when(nxt2 >= 0)
            def _():
                fetch(j, nxt2, slot).start()

    for_subs(nb, lambda b: x_copy(r0_ref[v], b, slot).wait())
    for_subs(nbn_ref[v], lambda b: x_copy(r0n_ref[v], b, 1 - slot).start())

    def compute(n_sub):
        M = n_sub * sub
        xl, xh = _unpack_halves(_load_row_tiles(xbuf_ref.at[slot, pl.ds(0, M)], S))
        xl, xh = xl.astype(BF16), xh.astype(BF16)
        gate = _dot(xl, wgb_ref[0:half, :]) + _dot(xh, wgb_ref[half:, :])
        up = _dot(xl, wub_ref[0:half, :]) + _dot(xh, wub_ref[half:, :])
        act = (_silu(gate) * up).astype(BF16)
        y = _dot(act, wdb_ref[...])
        _store_row_tiles(ybuf_ref.at[slot, pl.ds(0, M)], _pack_halves(y[:, :half], y[:, half:]))

    for n_sub in range(1, GMM_MAX_SUB + 1):
        pl.when(nb == n_sub)(functools.partial(compute, n_sub))

    for_subs(nbp_ref[v], lambda b: y_copy(0, b, 1 - slot).wait())
    for_subs(nb, lambda b: y_copy(r0_ref[v], b, slot).start())

    @pl.when(v == last)
    def _():
        for_subs(nb, lambda b: y_copy(0, b, slot).wait())
        ybuf_ref[0, pl.ds(0, sub), :] = jnp.zeros((sub, LANES), U32)
        pad = y_copy(ys_hbm.shape[0] // S - MOE_BLK, 0, 0)
        pad.start()
        pad.wait()


def _gmm(meta, xs, w_gate, w_up, w_down):
    E, D, FF = w_gate.shape
    half = D // 2
    S = half // LANES
    NI = meta[0].shape[0]
    n_meta = len(meta)
    hbm = pl.BlockSpec(memory_space=pl.ANY)
    item_rows = GMM_MAX_SUB * MOE_BLK
    vmem = (6 * _nbytes((D, FF), F32) + 3 * _nbytes((D, FF), BF16) + 4 * _nbytes((item_rows, half), U32)
            + 4 * _nbytes((item_rows, D), F32) + (6 << 20))
    return pl.pallas_call(
        _gmm_body,
        grid_spec=pltpu.PrefetchScalarGridSpec(
            num_scalar_prefetch=n_meta,
            grid=(NI,),
            in_specs=[hbm, hbm, hbm, hbm],
            out_specs=hbm,
            scratch_shapes=[pltpu.VMEM((2, D, FF), F32), pltpu.VMEM((2, D, FF), F32), pltpu.VMEM((2, FF, D), F32),
                            pltpu.VMEM((D, FF), BF16), pltpu.VMEM((D, FF), BF16), pltpu.VMEM((FF, D), BF16),
                            pltpu.VMEM((2, item_rows * S, LANES), U32), pltpu.VMEM((2, item_rows * S, LANES), U32),
                            pltpu.SemaphoreType.DMA((2, 3)), pltpu.SemaphoreType.DMA((2,)),
                            pltpu.SemaphoreType.DMA((2,))]),
        out_shape=jax.ShapeDtypeStruct(xs.shape, U32),
        compiler_params=_cparams(1, vmem),
        name="gmm",
    )(*meta, xs, w_gate, w_up, w_down)


def _gmm_meta(counts, n_rows):
    E = counts.shape[0]
    item_rows = GMM_MAX_SUB * MOE_BLK
    NV = E + n_rows // item_rows
    eids = jnp.arange(E, dtype=I32)
    ends = jnp.cumsum(counts)
    starts = ends - counts
    n_vis = (counts + item_rows - 1) // item_rows
    v_end = jnp.cumsum(n_vis)
    v_start = v_end - n_vis
    total = v_end[-1]
    later = jnp.where((eids[None, :] > eids[:, None]) & (counts[None, :] > 0), eids[None, :], E)
    nxt = jnp.min(later, axis=1)
    nxt2 = jnp.min(jnp.where(eids[None, :] > nxt[:, None], later, E), axis=1)
    nxt = jnp.where(nxt == E, -1, nxt)
    nxt2 = jnp.where(nxt2 == E, -1, nxt2)
    order = jnp.cumsum((counts > 0).astype(I32)) - 1
    v = jnp.arange(NV, dtype=I32)
    vc = jnp.minimum(v, total - 1)
    grp = jnp.sum((v_end[None, :] <= vc[:, None]).astype(I32), axis=1)
    onehot = grp[:, None] == eids[None, :]
    pick = lambda a: jnp.sum(jnp.where(onehot, a[None, :], 0), axis=1)
    valid = v < total
    part = vc - pick(v_start)
    r0 = pick(starts) + part * item_rows
    n_sub = jnp.where(valid, (jnp.minimum(pick(ends) - r0, item_rows) + MOE_BLK - 1) // MOE_BLK, 0)
    gfirst = (part == 0) & valid
    nxt_item = lambda a: jnp.concatenate([a[1:], jnp.zeros((1,), a.dtype)])
    prv_item = lambda a: jnp.concatenate([jnp.zeros((1,), a.dtype), a[:-1]])
    meta = (grp, r0, n_sub, gfirst, pick(order) % 2, pick(nxt), pick(nxt2), nxt_item(r0), nxt_item(n_sub),
            prv_item(n_sub))
    return tuple(m.astype(I32) for m in meta), starts


def _shared_body(h_ref, wg_ref, wu_ref, wd_ref, o_ref):
    h = h_ref[...]
    act = (_silu(_dot(h, wg_ref[...])) * _dot(h, wu_ref[...])).astype(BF16)
    o_ref[...] = _dot(act, wd_ref[...]).astype(o_ref.dtype)


def _shared(h2b, wg_bf, wu_bf, wd_bf):
    T, D = h2b.shape
    SF = wg_bf.shape[1]
    tm = min(512, T)
    full = lambda a, b: pl.BlockSpec((a, b), lambda i: (0, 0))
    vmem = 6 * _nbytes((D, SF), BF16) + 4 * _nbytes((tm, D), BF16) + 3 * _nbytes((tm, D), F32) + (4 << 20)
    return pl.pallas_call(
        _shared_body,
        grid=(T // tm,),
        in_specs=[pl.BlockSpec((tm, D), lambda i: (i, 0)), full(D, SF), full(D, SF), full(SF, D)],
        out_specs=pl.BlockSpec((tm, D), lambda i: (i, 0)),
        out_shape=jax.ShapeDtypeStruct((T, D), BF16),
        compiler_params=_cparams(1, vmem),
        name="shared",
    )(h2b, wg_bf, wu_bf, wd_bf)


def _combine_body(dest0_ref, destn_ref, ys_hbm, gw_ref, sh_ref, x1_ref, gt_ref, gpost_ref, o_ref, ybuf_ref, sems):
    i = pl.program_id(0)
    last = pl.num_programs(0) - 1
    D = x1_ref.shape[1]
    half = D // 2
    S = half // LANES

    def gather(dest_ref, slot):
        for t in range(ROW_TILE):
            for k in range(TOP_K):
                _row_copy(ys_hbm, dest_ref[0, 0, t * TOP_K + k], ybuf_ref.at[slot, k], t,
                          sems.at[slot], S).start(priority=k % 2)

    def drain(slot):
        for _ in range(ROW_TILE * TOP_K):
            _row_copy(ys_hbm, 0, ybuf_ref.at[slot, 0], 0, sems.at[slot], S).wait()

    @pl.when(i == 0)
    def _():
        gather(dest0_ref, 0)

    gather(destn_ref, (i + 1) % 2)
    drain(i % 2)
    slot = i % 2

    RB = COMBINE_ROWS
    for rb in range(ROW_TILE // RB):
        rows = slice(rb * RB, (rb + 1) * RB)
        wb = [jnp.broadcast_to(gw_ref[rows, k:k + 1], (RB, LANES)) for k in range(TOP_K)]
        ssq = jnp.zeros((RB, 1), F32)
        for s in range(S):
            c_lo = slice(s * LANES, (s + 1) * LANES)
            c_hi = slice(half + s * LANES, half + (s + 1) * LANES)
            fl = sh_ref[rows, c_lo].astype(F32)
            fh = sh_ref[rows, c_hi].astype(F32)
            for k in range(TOP_K):
                yw = ybuf_ref[slot, k, pl.ds(rb * RB * S + s, RB, stride=S), :]
                yl, yh = _unpack_halves(yw)
                fl = fl + yl * wb[k]
                fh = fh + yh * wb[k]
            ssq = ssq + jnp.sum(fl * fl, axis=-1, keepdims=True) + jnp.sum(fh * fh, axis=-1, keepdims=True)
            o_ref[rows, c_lo] = fl
            o_ref[rows, c_hi] = fh
        r = lax.rsqrt(ssq / D + NORM_EPS)
        o_ref[rows, :] = x1_ref[rows, :] + gt_ref[...] * ((o_ref[rows, :] * r) * gpost_ref[...])

    @pl.when(i == last)
    def _():
        drain((i + 1) % 2)


def _combine(dest_tiles, ys, gw, shared, x1, gt, gpost):
    T, D = x1.shape
    half = D // 2
    n_tiles = T // ROW_TILE
    row = lambda w: pl.BlockSpec((ROW_TILE, w), lambda i: (i, 0))
    vec = pl.BlockSpec((1, D), lambda i: (0, 0))
    idx = lambda f: pl.BlockSpec((1, 1, ROW_TILE * TOP_K), f, memory_space=pltpu.SMEM)
    S = half // LANES
    vmem = (2 * _nbytes((TOP_K, ROW_TILE, half), U32) + 4 * _nbytes((ROW_TILE, D), F32)
            + 2 * _nbytes((ROW_TILE, D), BF16) + 8 * _nbytes((ROW_TILE, D), F32) + (4 << 20))
    return pl.pallas_call(
        _combine_body,
        grid=(n_tiles,),
        in_specs=[idx(lambda i: (0, 0, 0)), idx(lambda i: (jnp.minimum(i + 1, n_tiles - 1), 0, 0)),
                  pl.BlockSpec(memory_space=pl.ANY), row(TOP_K), row(D), row(D), vec, vec],
        out_specs=row(D),
        out_shape=jax.ShapeDtypeStruct((T, D), F32),
        scratch_shapes=[pltpu.VMEM((2, TOP_K, ROW_TILE * S, LANES), U32), pltpu.SemaphoreType.DMA((2,))],
        compiler_params=_cparams(1, vmem),
        name="combine",
    )(dest_tiles, dest_tiles, ys, gw, shared, x1, gt, gpost)


def _layer(x, c, pos_f, w_ada, b_ada, g_mix_pre, g_mix_post, g_ffn_pre, g_ffn_post, w_in, conv_w, ret_gn_w,
           w_out, w_router, router_bias, w_gate, w_up, w_down, ws_gate, ws_up, ws_down):
    T, D = x.shape
    CW = conv_w.shape[-1]
    RW = ret_gn_w.shape[-1]
    assert CW == RW and w_in.shape[1] == 3 * CW + 4 * RW and w_out.shape[0] == CW + RW
    assert T % ROW_TILE == 0 and (T * TOP_K) % (GMM_MAX_SUB * MOE_BLK) == 0 and D % (2 * LANES) == 0
    assert MOE_BLK <= ROW_TILE
    vec = lambda a: a.reshape(1, D)

    mod = _ada(c, w_ada, b_ada)
    sh1, sc1, gt1, sh2, sc2, gt2 = (mod[:, i * D:(i + 1) * D] for i in range(6))

    proj = _inproj(x, vec(g_mix_pre), sc1, sh1, w_in.astype(BF16), tn=CW)
    y_conv = _conv(proj, conv_w, CW)
    y_ret = _retention(proj, pos_f, ret_gn_w.reshape(1, RW), RW, col0_blocks=3)
    x1, h2b, h2p, logits_t = _outproj(y_conv, y_ret, w_out.astype(BF16), x, vec(g_mix_post), gt1,
                                      vec(g_ffn_pre), sc2, sh2, w_router)

    eidx_t, gw_t, rank_t, cnt = _route(logits_t, router_bias)
    counts = cnt[:, 0].astype(I32)
    meta, starts = _gmm_meta(counts, T * TOP_K)
    dest = _dest(eidx_t, rank_t, starts, D // 2 // LANES).T
    dest_tiles = dest.reshape(T // ROW_TILE, 1, ROW_TILE * TOP_K)

    xs = _dispatch(dest_tiles, h2p, D // 2 // LANES)
    ys = _gmm(meta, xs, w_gate, w_up, w_down)
    shared = _shared(h2b, ws_gate.astype(BF16), ws_up.astype(BF16), ws_down.astype(BF16))
    return _combine(dest_tiles, ys, gw_t.T, shared, x1, gt2, vec(g_ffn_post))


def kernel(x, c, positions, w_ada, b_ada, g_mix_pre, g_mix_post, g_ffn_pre, g_ffn_post, w_in, conv_w, ret_gn_w,
           w_out, w_router, router_bias, w_gate, w_up, w_down, ws_gate, ws_up, ws_down):
    B, S, D = x.shape
    assert B == 1, "one sequence per call"
    xt = x.reshape(S, D)
    pos_f = positions.astype(F32).reshape(S, 1)
    for l in range(w_ada.shape[0]):
        xt = _layer(xt, c, pos_f, w_ada[l], b_ada[l], g_mix_pre[l], g_mix_post[l], g_ffn_pre[l], g_ffn_post[l],
                    w_in[l], conv_w[l], ret_gn_w[l], w_out[l], w_router[l], router_bias[l], w_gate[l], w_up[l],
                    w_down[l], ws_gate[l], ws_up[l], ws_down[l])
    return xt.reshape(B, S, D)
```

```python
import functools
import math

import jax
import jax.numpy as jnp
from jax import lax
from jax.experimental import pallas as pl
from jax.experimental.pallas import tpu as pltpu

F32 = jnp.float32
BF16 = jnp.bfloat16
U32 = jnp.uint32
I32 = jnp.int32

CONV_K = 3
RET_HEAD_DIM = 128
ROPE_THETA = 10000.0
TOP_K = 8
N_GROUPS = 8
TOPK_GROUPS = 4
ROUTED_SCALE = 2.5
NORM_EPS = 1e-6
GN_EPS = 1e-5

LANES = 128
SUBLANES = 8
VMEM_LIMIT_CAP = 56 * 1024 * 1024

RET_CHUNK = 256
MOE_BLK = 128
GMM_MAX_SUB = 4
ROW_TILE = 128
OUTPROJ_SUB = 256
COMBINE_ROWS = 32
INPROJ_PARTS = 4


def _cparams(n_axes, vmem_bytes):
    return pltpu.CompilerParams(
        dimension_semantics=("arbitrary",) * n_axes,
        vmem_limit_bytes=int(min(max(vmem_bytes, 16 * 1024 * 1024), VMEM_LIMIT_CAP)))


def _nbytes(shape, dtype):
    return math.prod(shape) * jnp.dtype(dtype).itemsize


def _dot(a, b):
    return jnp.dot(a, b, preferred_element_type=F32)


def _dot_nt(a, b):
    return lax.dot_general(a, b, (((1,), (1,)), ((), ())), preferred_element_type=F32)


def _dot_tn(a, b):
    return lax.dot_general(a, b, (((0,), (0,)), ((), ())), preferred_element_type=F32)


def _silu(v):
    return v * jax.nn.sigmoid(v)


def _pack_halves(lo, hi):
    lo_b = lax.bitcast_convert_type(lo.astype(BF16).astype(F32), U32)
    hi_b = lax.bitcast_convert_type(hi.astype(BF16).astype(F32), U32)
    return (lo_b >> 16) | (hi_b & jnp.uint32(0xFFFF0000))


def _unpack_halves(w):
    lo = lax.bitcast_convert_type(w << 16, F32)
    hi = lax.bitcast_convert_type(w & jnp.uint32(0xFFFF0000), F32)
    return lo, hi


def _store_row_tiles(ref, val):
    rows, S = val.shape[0], val.shape[1] // LANES
    for s in range(S):
        ref[pl.ds(s, rows, stride=S), :] = val[:, s * LANES:(s + 1) * LANES]


def _load_row_chunk(ref, s, S):
    return ref[pl.ds(s, ref.shape[0] // S, stride=S), :]


def _load_row_tiles(ref, S):
    return jnp.concatenate([_load_row_chunk(ref, s, S) for s in range(S)], axis=1)


def _ada_body(cb_ref, w_ref, b_ref, o_ref):
    s = _silu(cb_ref[...])
    for j in range(o_ref.shape[-1] // LANES):
        sl = slice(j * LANES, (j + 1) * LANES)
        o_ref[:, sl] = jnp.sum(w_ref[:, sl] * s, axis=0, keepdims=True) + b_ref[:, sl]


def _ada(c, w_ada, b_ada):
    D, N = w_ada.shape
    tn = min(1024, N)
    cb = jnp.broadcast_to(c.reshape(D, 1), (D, LANES))
    vmem = 2 * _nbytes((D, tn), F32) + 3 * _nbytes((D, LANES), F32) + (4 << 20)
    return pl.pallas_call(
        _ada_body,
        grid=(N // tn,),
        in_specs=[pl.BlockSpec((D, LANES), lambda j: (0, 0)),
                  pl.BlockSpec((D, tn), lambda j: (0, j)),
                  pl.BlockSpec((1, tn), lambda j: (0, j))],
        out_specs=pl.BlockSpec((1, tn), lambda j: (0, j)),
        out_shape=jax.ShapeDtypeStruct((1, N), F32),
        compiler_params=_cparams(1, vmem),
        name="ada",
    )(cb, w_ada, b_ada.reshape(1, N))


def _inproj_body(x_ref, g_ref, sc_ref, sh_ref, w_ref, o_ref, h_ref):
    tm = x_ref.shape[0]
    part = tm // INPROJ_PARTS

    @pl.when(pl.program_id(1) == 0)
    def _():
        for p in range(INPROJ_PARTS):
            rows = slice(p * part, (p + 1) * part)
            xf = x_ref[rows, :]
            r = xf * lax.rsqrt(jnp.mean(xf * xf, axis=-1, keepdims=True) + NORM_EPS)
            h = ((r * g_ref[...]) * (1.0 + sc_ref[...]) + sh_ref[...]).astype(BF16)
            h_ref[rows, :] = h
            o_ref[rows, :] = _dot(h, w_ref[...]).astype(o_ref.dtype)

    @pl.when(pl.program_id(1) != 0)
    def _():
        o_ref[...] = _dot(h_ref[...], w_ref[...]).astype(o_ref.dtype)


def _inproj(x, g, sc, sh, w_bf, tn):
    T, D = x.shape
    N = w_bf.shape[1]
    tm = min(1024, T)
    vec = pl.BlockSpec((1, D), lambda i, j: (0, 0))
    vmem = (2 * _nbytes((tm, D), F32) + _nbytes((tm, D), BF16) + 2 * _nbytes((D, tn), BF16)
            + 2 * _nbytes((tm, tn), BF16) + 2 * _nbytes((tm, tn), F32) + 2 * _nbytes((tm, D), F32) + (4 << 20))
    return pl.pallas_call(
        _inproj_body,
        grid=(T // tm, N // tn),
        in_specs=[pl.BlockSpec((tm, D), lambda i, j: (i, 0)), vec, vec, vec,
                  pl.BlockSpec((D, tn), lambda i, j: (0, j))],
        out_specs=pl.BlockSpec((tm, tn), lambda i, j: (i, j)),
        out_shape=jax.ShapeDtypeStruct((T, N), BF16),
        scratch_shapes=[pltpu.VMEM((tm, D), BF16)],
        compiler_params=_cparams(2, vmem),
        name="inproj",
    )(x, g, sc, sh, w_bf)


def _conv_body(cb_ref, cc_ref, cx_ref, w_ref, o_ref, u_ref):
    tc = cb_ref.shape[0]

    @pl.when(pl.program_id(1) == 0)
    def _():
        u_ref[0:SUBLANES, :] = jnp.zeros((SUBLANES, u_ref.shape[1]), F32)

    u = cc_ref[...].astype(F32) * cx_ref[...].astype(F32)
    u_ref[SUBLANES:SUBLANES + tc, :] = u
    u1 = u_ref[SUBLANES - 1:SUBLANES - 1 + tc, :]
    u2 = u_ref[SUBLANES - 2:SUBLANES - 2 + tc, :]
    conv = u2 * w_ref[0:1, :] + u1 * w_ref[1:2, :] + u * w_ref[2:3, :]
    o_ref[...] = (cb_ref[...].astype(F32) * conv).astype(o_ref.dtype)
    u_ref[0:SUBLANES, :] = u_ref[tc:tc + SUBLANES, :]


def _conv(proj, conv_w, CW):
    T = proj.shape[0]
    tc = min(512, T)
    tw = min(512, CW)
    nb = CW // tw
    vmem = 8 * _nbytes((tc, tw), BF16) + 8 * _nbytes((tc + SUBLANES, tw), F32) + (4 << 20)
    return pl.pallas_call(
        _conv_body,
        grid=(nb, T // tc),
        in_specs=[pl.BlockSpec((tc, tw), lambda j, i: (i, j)),
                  pl.BlockSpec((tc, tw), lambda j, i: (i, nb + j)),
                  pl.BlockSpec((tc, tw), lambda j, i: (i, 2 * nb + j)),
                  pl.BlockSpec((CONV_K, tw), lambda j, i: (0, j))],
        out_specs=pl.BlockSpec((tc, tw), lambda j, i: (i, j)),
        out_shape=jax.ShapeDtypeStruct((T, CW), BF16),
        scratch_shapes=[pltpu.VMEM((tc + SUBLANES, tw), F32)],
        compiler_params=_cparams(2, vmem),
        name="conv",
    )(proj, proj, proj, conv_w)


def _ret_body(q_ref, k_ref, v_ref, g_ref, pos_ref, inv_ref, gnw_ref, o_ref, state_ref, dec_ref, *, H, C):
    dh = RET_HEAD_DIM
    log_gamma = [math.log1p(-(2.0 ** (-5.0 - h))) for h in range(H)]

    @pl.when(pl.program_id(0) == 0)
    def _():
        state_ref[...] = jnp.zeros_like(state_ref)
        rel = (lax.broadcasted_iota(I32, (C, C), 0) - lax.broadcasted_iota(I32, (C, C), 1)).astype(F32)
        for h in range(H):
            dec_ref[h] = jnp.where(rel >= 0.0, jnp.exp(log_gamma[h] * jnp.maximum(rel, 0.0)), 0.0)

    ang = pos_ref[...] * inv_ref[...]
    cos2 = jnp.cos(ang)
    sin2 = jnp.where(lax.broadcasted_iota(I32, (C, dh), 1) < dh // 2, -jnp.sin(ang), jnp.sin(ang))
    ridx = lax.broadcasted_iota(I32, (C, 1), 0).astype(F32)

    def rope(t):
        return t * cos2 + pltpu.roll(t, dh // 2, 1) * sin2

    for h in range(H):
        sl = slice(h * dh, (h + 1) * dh)
        lg = log_gamma[h]
        qr = rope(q_ref[:, sl].astype(F32))
        kr = rope(k_ref[:, sl].astype(F32)) * (dh ** -0.5)
        v = v_ref[:, sl]
        q_decay = jnp.exp(lg * (ridx + 1.0))
        k_decay = jnp.exp(lg * (C - 1.0 - ridx))
        scores = _dot_nt(qr.astype(BF16), kr.astype(BF16)) * dec_ref[h]
        state = state_ref[h]
        o = _dot(scores.astype(BF16), v) + _dot((qr * q_decay).astype(BF16), state.astype(BF16))
        state_ref[h] = state * math.exp(lg * C) + _dot_tn((kr * k_decay).astype(BF16), v)
        mu = jnp.mean(o, axis=-1, keepdims=True)
        d = o - mu
        var = jnp.mean(d * d, axis=-1, keepdims=True)
        on = d * lax.rsqrt(var + GN_EPS) * gnw_ref[:, sl]
        o_ref[:, sl] = (_silu(g_ref[:, sl].astype(F32)) * on).astype(o_ref.dtype)


def _retention(proj, pos_f, gn_w, RW, col0_blocks):
    T = proj.shape[0]
    C = min(RET_CHUNK, T)
    H = RW // RET_HEAD_DIM
    half = RET_HEAD_DIM // 2
    inv = ROPE_THETA ** (-jnp.arange(half, dtype=F32) / half)
    inv2 = jnp.concatenate([inv, inv]).reshape(1, RET_HEAD_DIM)
    blk = lambda off: pl.BlockSpec((C, RW), lambda i: (i, col0_blocks + off))
    vmem = (10 * _nbytes((C, RW), BF16) + _nbytes((H, RET_HEAD_DIM, RET_HEAD_DIM), F32)
            + _nbytes((H, C, C), F32) + 24 * _nbytes((C, C), F32) + (8 << 20))
    return pl.pallas_call(
        functools.partial(_ret_body, H=H, C=C),
        grid=(T // C,),
        in_specs=[blk(0), blk(1), blk(2), blk(3),
                  pl.BlockSpec((C, 1), lambda i: (i, 0)),
                  pl.BlockSpec((1, RET_HEAD_DIM), lambda i: (0, 0)),
                  pl.BlockSpec((1, RW), lambda i: (0, 0))],
        out_specs=pl.BlockSpec((C, RW), lambda i: (i, 0)),
        out_shape=jax.ShapeDtypeStruct((T, RW), BF16),
        scratch_shapes=[pltpu.VMEM((H, RET_HEAD_DIM, RET_HEAD_DIM), F32), pltpu.VMEM((H, C, C), F32)],
        compiler_params=_cparams(1, vmem),
        name="ret",
    )(proj, proj, proj, proj, pos_f, inv2, gn_w)


def _outproj_body(yc_ref, yr_ref, w_ref, x_ref, gpost_ref, gt_ref, gpre_ref, sc_ref, sh_ref, wr_ref,
                  x1_ref, h2p_ref, lt_ref, wrhi_ref, wrlo_ref):
    CW = yc_ref.shape[1]
    D = x_ref.shape[1]

    @pl.when(pl.program_id(0) == 0)
    def _():
        w = wr_ref[...]
        hi = w.astype(BF16)
        wrhi_ref[...] = hi
        wrlo_ref[...] = (w - hi.astype(F32)).astype(BF16)

    S = D // 2 // LANES
    tm = x_ref.shape[0]
    sub = min(tm, OUTPROJ_SUB)
    for r0 in range(0, tm, sub):
        rows = slice(r0, r0 + sub)
        m = _dot(yc_ref[rows, :], w_ref[0:CW, :]) + _dot(yr_ref[rows, :], w_ref[CW:, :])
        mn = (m * lax.rsqrt(jnp.mean(m * m, axis=-1, keepdims=True) + NORM_EPS)) * gpost_ref[...]
        x1 = x_ref[rows, :] + gt_ref[...] * mn
        x1_ref[rows, :] = x1
        r = x1 * lax.rsqrt(jnp.mean(x1 * x1, axis=-1, keepdims=True) + NORM_EPS)
        h2 = (r * gpre_ref[...]) * (1.0 + sc_ref[...]) + sh_ref[...]
        hb = h2.astype(BF16)
        _store_row_tiles(h2p_ref.at[pl.ds(r0 * S, sub * S)], _pack_halves(h2[:, :D // 2], h2[:, D // 2:]))
        hl = (h2 - hb.astype(F32)).astype(BF16)
        lt_ref[:, rows] = (_dot_nt(wrhi_ref[...], hb) + _dot_nt(wrhi_ref[...], hl)) + _dot_nt(wrlo_ref[...], hb)


def _outproj(yc, yr, w_out_bf, x, gpost, gt, gpre, sc, sh, w_router):
    T, D = x.shape
    CW, RW = yc.shape[1], yr.shape[1]
    E = w_router.shape[0]
    S = D // 2 // LANES
    tm = min(2 * OUTPROJ_SUB, T)
    vec = pl.BlockSpec((1, D), lambda i: (0, 0))
    row = lambda w: pl.BlockSpec((tm, w), lambda i: (i, 0))
    const = lambda a, b: pl.BlockSpec((a, b), lambda i: (0, 0), pipeline_mode=pl.Buffered(1))
    vmem = (_nbytes((CW + RW, D), BF16) + _nbytes((E, D), F32) + 2 * _nbytes((E, D), BF16)
            + 4 * _nbytes((tm, D), F32) + 4 * _nbytes((tm, D), BF16) + 2 * _nbytes((tm, D // 2), U32)
            + 2 * _nbytes((E, tm), F32) + 5 * _nbytes((tm, D), F32) + (4 << 20))
    return pl.pallas_call(
        _outproj_body,
        grid=(T // tm,),
        in_specs=[row(CW), row(RW), const(CW + RW, D), row(D),
                  vec, vec, vec, vec, vec, const(E, D)],
        out_specs=[row(D), pl.BlockSpec((tm * S, LANES), lambda i: (i, 0)),
                   pl.BlockSpec((E, tm), lambda i: (0, i))],
        out_shape=[jax.ShapeDtypeStruct((T, D), F32),
                   jax.ShapeDtypeStruct((T * S, LANES), U32), jax.ShapeDtypeStruct((E, T), F32)],
        scratch_shapes=[pltpu.VMEM((E, D), BF16), pltpu.VMEM((E, D), BF16)],
        compiler_params=_cparams(1, vmem),
        name="outproj",
    )(yc, yr, w_out_bf, x, gpost, gt, gpre, sc, sh, w_router)


def _route_body(lt_ref, bias_ref, eidx_ref, gw_ref, rank_ref, cnt_ref, carry_ref):
    E, TL = lt_ref.shape
    G, EG = N_GROUPS, E // N_GROUPS
    NEG, BIG = -jnp.inf, 1e9

    @pl.when(pl.program_id(0) == 0)
    def _():
        carry_ref[...] = jnp.zeros_like(carry_ref)

    scores = jax.nn.sigmoid(lt_ref[...]).reshape(G, EG, TL)
    biased = scores + bias_ref[...].reshape(G, EG, 1)
    eio = lax.broadcasted_iota(I32, (G, EG, TL), 1).astype(F32)
    eid = lax.broadcasted_iota(I32, (G, EG, TL), 0).astype(F32) * EG + eio
    gio = lax.broadcasted_iota(I32, (G, 1, TL), 0).astype(F32)

    m1 = jnp.max(biased, axis=1, keepdims=True)
    i1 = jnp.min(jnp.where(biased == m1, eio, BIG), axis=1, keepdims=True)
    m2 = jnp.max(jnp.where(eio == i1, NEG, biased), axis=1, keepdims=True)
    cur = m1 + m2
    gmask = jnp.zeros((G, 1, TL), F32)
    for _ in range(TOPK_GROUPS):
        mm = jnp.max(cur, axis=0, keepdims=True)
        ii = jnp.min(jnp.where(cur == mm, gio, BIG), axis=0, keepdims=True)
        sel = gio == ii
        gmask = jnp.where(sel, 1.0, gmask)
        cur = jnp.where(sel, NEG, cur)

    cur = jnp.where(gmask > 0.0, biased, NEG)
    sels, idxs, ws = [], [], []
    for _ in range(TOP_K):
        mm = jnp.max(jnp.max(cur, axis=1, keepdims=True), axis=0, keepdims=True)
        cand = jnp.where(cur == mm, eid, BIG)
        ii = jnp.min(jnp.min(cand, axis=1, keepdims=True), axis=0, keepdims=True)
        sel = eid == ii
        w = jnp.where(sel, scores, 0.0)
        ws.append(jnp.sum(jnp.sum(w, axis=1, keepdims=True), axis=0, keepdims=True).reshape(1, TL))
        idxs.append(ii.reshape(1, TL))
        sels.append(sel)
        cur = jnp.where(sel, NEG, cur)

    wsum = ws[0]
    for w in ws[1:]:
        wsum = wsum + w
    for k in range(TOP_K):
        eidx_ref[k:k + 1, :] = idxs[k].astype(I32)
        gw_ref[k:k + 1, :] = ws[k] / wsum * ROUTED_SCALE

    chosen = jnp.zeros((G, EG, TL), F32)
    for sel in sels:
        chosen = jnp.where(sel, 1.0, chosen)
    chosen2 = chosen.reshape(E, TL).astype(BF16)
    before = (lax.broadcasted_iota(I32, (TL, TL), 0) < lax.broadcasted_iota(I32, (TL, TL), 1))
    prefix = _dot(chosen2, jnp.where(before, 1.0, 0.0).astype(BF16))
    total = _dot(chosen2, jnp.ones((TL, TL), BF16))
    carry = carry_ref[...]
    rank_full = (carry + prefix).reshape(G, EG, TL)
    for k in range(TOP_K):
        r = jnp.where(sels[k], rank_full, 0.0)
        r = jnp.sum(jnp.sum(r, axis=1, keepdims=True), axis=0, keepdims=True).reshape(1, TL)
        rank_ref[k:k + 1, :] = r.astype(I32)
    carry = carry + total
    carry_ref[...] = carry
    cnt_ref[...] = carry[:, 0:LANES]


def _route(logits_t, bias):
    E, T = logits_t.shape
    TL = min(256, T)
    tok = pl.BlockSpec((TOP_K, TL), lambda i: (0, i))
    vmem = 64 * _nbytes((E, TL), F32) + (8 << 20)
    return pl.pallas_call(
        _route_body,
        grid=(T // TL,),
        in_specs=[pl.BlockSpec((E, TL), lambda i: (0, i)), pl.BlockSpec((E, 1), lambda i: (0, 0))],
        out_specs=[tok, tok, tok, pl.BlockSpec((E, LANES), lambda i: (0, 0))],
        out_shape=[jax.ShapeDtypeStruct((TOP_K, T), I32), jax.ShapeDtypeStruct((TOP_K, T), F32),
                   jax.ShapeDtypeStruct((TOP_K, T), I32), jax.ShapeDtypeStruct((E, LANES), F32)],
        scratch_shapes=[pltpu.VMEM((E, TL), F32)],
        compiler_params=_cparams(1, vmem),
        name="route",
    )(logits_t, bias.reshape(E, 1))


def _dest_body(eidx_ref, rank_ref, starts_ref, dest_ref, *, S):
    E, TL = starts_ref.shape[0], eidx_ref.shape[1]
    eid = lax.broadcasted_iota(I32, (E, TL), 0)
    starts = starts_ref[...]
    for k in range(TOP_K):
        base = jnp.sum(jnp.where(eid == eidx_ref[k:k + 1, :], starts, 0.0), axis=0, keepdims=True)
        dest_ref[k:k + 1, :] = (base.astype(I32) + rank_ref[k:k + 1, :]) * S


def _dest(eidx_t, rank_t, starts, S):
    K, T = eidx_t.shape
    E = starts.shape[0]
    TL = min(512, T)
    tok = pl.BlockSpec((K, TL), lambda i: (0, i))
    return pl.pallas_call(
        functools.partial(_dest_body, S=S),
        grid=(T // TL,),
        in_specs=[tok, tok, pl.BlockSpec((E, 1), lambda i: (0, 0))],
        out_specs=tok,
        out_shape=jax.ShapeDtypeStruct((K, T), I32),
        compiler_params=_cparams(1, 16 * _nbytes((E, TL), F32) + (4 << 20)),
        name="dest",
    )(eidx_t, rank_t, starts.astype(F32).reshape(E, 1))


def _row_span(r, S):
    if isinstance(r, int):
        return pl.ds(r * S, S)
    return pl.ds(pl.multiple_of(r, S), S)


def _row_copy(src, s, dst, d, sem, S):
    return pltpu.make_async_copy(src.at[_row_span(s, S)], dst.at[_row_span(d, S)], sem)


def _dispatch_body(dest_ref, h_hbm, wg_ref, wu_ref, wd_ref, xs_hbm, sh_ref, hbuf_ref, load_sems, sems, *, S):
    i = pl.program_id(0)
    last = pl.num_programs(0) - 1
    slot = i % 2
    tile_rows = ROW_TILE * S

    def load(tile, dst_slot):
        src = h_hbm.at[pl.ds(pl.multiple_of(tile * tile_rows, tile_rows), tile_rows)]
        return pltpu.make_async_copy(src, hbuf_ref.at[dst_slot], load_sems.at[dst_slot])

    def drain(s):
        for _ in range(ROW_TILE * TOP_K):
            _row_copy(hbuf_ref.at[s], 0, xs_hbm, 0, sems.at[s], S).wait()

    @pl.when(i == 0)
    def _():
        load(0, 0).start()
        pad_rows = MOE_BLK * S
        hbuf_ref[1, pl.ds(0, pad_rows), :] = jnp.zeros((pad_rows, LANES), U32)
        pad = pltpu.make_async_copy(hbuf_ref.at[1, pl.ds(0, pad_rows)],
                                    xs_hbm.at[pl.ds(xs_hbm.shape[0] - pad_rows, pad_rows)], sems.at[1])
        pad.start()
        pad.wait()

    @pl.when(i > 0)
    def _():
        drain(1 - slot)

    @pl.when(i < last)
    def _():
        load(i + 1, 1 - slot).start()

    load(i, slot).wait()
    for t in range(ROW_TILE):
        for k in range(TOP_K):
            _row_copy(hbuf_ref.at[slot], t, xs_hbm, dest_ref[0, 0, t * TOP_K + k], sems.at[slot],
                      S).start(priority=k % 2)

    half = S * LANES
    xl, xh = _unpack_halves(_load_row_tiles(hbuf_ref.at[slot], S))
    xl, xh = xl.astype(BF16), xh.astype(BF16)
    gate = _dot(xl, wg_ref[0:half, :]) + _dot(xh, wg_ref[half:, :])
    up = _dot(xl, wu_ref[0:half, :]) + _dot(xh, wu_ref[half:, :])
    sh_ref[...] = _dot((_silu(gate) * up).astype(BF16), wd_ref[...]).astype(sh_ref.dtype)

    @pl.when(i == last)
    def _():
        drain(slot)


def _dispatch(dest_tiles, h2p, wg_bf, wu_bf, wd_bf, S):
    T = h2p.shape[0] // S
    D, SF = wg_bf.shape
    n_tiles = T // ROW_TILE
    const = lambda a, b: pl.BlockSpec((a, b), lambda i: (0, 0), pipeline_mode=pl.Buffered(1))
    vmem = (3 * _nbytes((D, SF), BF16) + 2 * _nbytes((ROW_TILE, D // 2), U32) + 2 * _nbytes((ROW_TILE, D), BF16)
            + 6 * _nbytes((ROW_TILE, D), F32) + (4 << 20))
    return pl.pallas_call(
        functools.partial(_dispatch_body, S=S),
        grid=(n_tiles,),
        in_specs=[pl.BlockSpec((1, 1, ROW_TILE * TOP_K), lambda i: (i, 0, 0), memory_space=pltpu.SMEM),
                  pl.BlockSpec(memory_space=pl.ANY), const(D, SF), const(D, SF), const(SF, D)],
        out_specs=[pl.BlockSpec(memory_space=pl.ANY), pl.BlockSpec((ROW_TILE, D), lambda i: (i, 0))],
        out_shape=[jax.ShapeDtypeStruct(((T * TOP_K + MOE_BLK) * S, LANES), U32),
                   jax.ShapeDtypeStruct((T, D), BF16)],
        scratch_shapes=[pltpu.VMEM((2, ROW_TILE * S, LANES), U32), pltpu.SemaphoreType.DMA((2,)),
                        pltpu.SemaphoreType.DMA((2,))],
        compiler_params=_cparams(1, vmem),
        name="dispatch",
    )(dest_tiles, h2p, wg_bf, wu_bf, wd_bf)


def _gmm_body(grp_ref, r0_ref, nb_ref, gfirst_ref, gslot_ref, gnext_ref, gnext2_ref, r0n_ref, nbn_ref, nbp_ref,
              xs_hbm, wg_hbm, wu_hbm, wd_hbm, ys_hbm,
              wgf_ref, wuf_ref, wdf_ref, wgb_ref, wub_ref, wdb_ref, xbuf_ref, ybuf_ref, sems, xsems, ysems):
    v = pl.program_id(0)
    last = pl.num_programs(0) - 1
    half = wgb_ref.shape[0] // 2
    S = half // LANES
    sub = MOE_BLK * S
    slot = v % 2
    nb = nb_ref[v]
    landing = ((wg_hbm, wgf_ref, wgb_ref), (wu_hbm, wuf_ref, wub_ref), (wd_hbm, wdf_ref, wdb_ref))

    def sub_rows(r0, b):
        first = (r0 + b * MOE_BLK) * S
        return pl.ds(first if isinstance(r0, int) else pl.multiple_of(first, S), sub)

    def x_copy(r0, b, s):
        return pltpu.make_async_copy(xs_hbm.at[sub_rows(r0, b)], xbuf_ref.at[s, pl.ds(b * sub, sub)], xsems.at[s])

    def y_copy(r0, b, s):
        return pltpu.make_async_copy(ybuf_ref.at[s, pl.ds(b * sub, sub)], ys_hbm.at[sub_rows(r0, b)], ysems.at[s])

    def for_subs(n, fn):
        for b in range(GMM_MAX_SUB):
            @pl.when(b < n)
            def _():
                fn(b)

    def fetch(j, g, slot):
        return pltpu.make_async_copy(landing[j][0].at[g], landing[j][1].at[slot], sems.at[slot, j])

    @pl.when(v == 0)
    def _():
        for j in range(3):
            fetch(j, grp_ref[0], 0).start()

        @pl.when(gnext_ref[0] >= 0)
        def _():
            for j in range(3):
                fetch(j, gnext_ref[0], 1).start()

        for_subs(nb, lambda b: x_copy(r0_ref[0], b, 0).start())

    @pl.when(gfirst_ref[v] == 1)
    def _():
        slot = gslot_ref[v]
        nxt2 = gnext2_ref[v]
        for j in range(3):
            fetch(j, grp_ref[v], slot).wait()
            landing[j][2][...] = landing[j][1][slot].astype(BF16)

            @pl.when(nxt2 >= 0)
            def _():
                fetch(j, nxt2, slot).start()

    for_subs(nb, lambda b: x_copy(r0_ref[v], b, slot).wait())
    for_subs(nbn_ref[v], lambda b: x_copy(r0n_ref[v], b, 1 - slot).start())

    def compute(n_sub):
        M = n_sub * sub
        xl, xh = _unpack_halves(_load_row_tiles(xbuf_ref.at[slot, pl.ds(0, M)], S))
        xl, xh = xl.astype(BF16), xh.astype(BF16)
        gate = _dot(xl, wgb_ref[0:half, :]) + _dot(xh, wgb_ref[half:, :])
        up = _dot(xl, wub_ref[0:half, :]) + _dot(xh, wub_ref[half:, :])
        act = (_silu(gate) * up).astype(BF16)
        y = _dot(act, wdb_ref[...])
        _store_row_tiles(ybuf_ref.at[slot, pl.ds(0, M)], _pack_halves(y[:, :half], y[:, half:]))

    for n_sub in range(1, GMM_MAX_SUB + 1):
        pl.when(nb == n_sub)(functools.partial(compute, n_sub))

    for_subs(nbp_ref[v], lambda b: y_copy(0, b, 1 - slot).wait())
    for_subs(nb, lambda b: y_copy(r0_ref[v], b, slot).start())

    @pl.when(v == last)
    def _():
        for_subs(nb, lambda b: y_copy(0, b, slot).wait())
        ybuf_ref[0, pl.ds(0, sub), :] = jnp.zeros((sub, LANES), U32)
        pad = y_copy(ys_hbm.shape[0] // S - MOE_BLK, 0, 0)
        pad.start()
        pad.wait()


def _gmm(meta, xs, w_gate, w_up, w_down):
    E, D, FF = w_gate.shape
    half = D // 2
    S = half // LANES
    NI = meta[0].shape[0]
    n_meta = len(meta)
    hbm = pl.BlockSpec(memory_space=pl.ANY)
    item_rows = GMM_MAX_SUB * MOE_BLK
    vmem = (6 * _nbytes((D, FF), F32) + 3 * _nbytes((D, FF), BF16) + 4 * _nbytes((item_rows, half), U32)
            + 4 * _nbytes((item_rows, D), F32) + (6 << 20))
    return pl.pallas_call(
        _gmm_body,
        grid_spec=pltpu.PrefetchScalarGridSpec(
            num_scalar_prefetch=n_meta,
            grid=(NI,),
            in_specs=[hbm, hbm, hbm, hbm],
            out_specs=hbm,
            scratch_shapes=[pltpu.VMEM((2, D, FF), F32), pltpu.VMEM((2, D, FF), F32), pltpu.VMEM((2, FF, D), F32),
                            pltpu.VMEM((D, FF), BF16), pltpu.VMEM((D, FF), BF16), pltpu.VMEM((FF, D), BF16),
                            pltpu.VMEM((2, item_rows * S, LANES), U32), pltpu.VMEM((2, item_rows * S, LANES), U32),
                            pltpu.SemaphoreType.DMA((2, 3)), pltpu.SemaphoreType.DMA((2,)),
                            pltpu.SemaphoreType.DMA((2,))]),
        out_shape=jax.ShapeDtypeStruct(xs.shape, U32),
        compiler_params=_cparams(1, vmem),
        name="gmm",
    )(*meta, xs, w_gate, w_up, w_down)


def _gmm_meta(counts, n_rows):
    E = counts.shape[0]
    item_rows = GMM_MAX_SUB * MOE_BLK
    NV = E + n_rows // item_rows
    eids = jnp.arange(E, dtype=I32)
    ends = jnp.cumsum(counts)
    starts = ends - counts
    n_vis = (counts + item_rows - 1) // item_rows
    v_end = jnp.cumsum(n_vis)
    v_start = v_end - n_vis
    total = v_end[-1]
    later = jnp.where((eids[None, :] > eids[:, None]) & (counts[None, :] > 0), eids[None, :], E)
    nxt = jnp.min(later, axis=1)
    nxt2 = jnp.min(jnp.where(eids[None, :] > nxt[:, None], later, E), axis=1)
    nxt = jnp.where(nxt == E, -1, nxt)
    nxt2 = jnp.where(nxt2 == E, -1, nxt2)
    order = jnp.cumsum((counts > 0).astype(I32)) - 1
    v = jnp.arange(NV, dtype=I32)
    vc = jnp.minimum(v, total - 1)
    grp = jnp.sum((v_end[None, :] <= vc[:, None]).astype(I32), axis=1)
    onehot = grp[:, None] == eids[None, :]
    pick = lambda a: jnp.sum(jnp.where(onehot, a[None, :], 0), axis=1)
    valid = v < total
    part = vc - pick(v_start)
    r0 = pick(starts) + part * item_rows
    n_sub = jnp.where(valid, (jnp.minimum(pick(ends) - r0, item_rows) + MOE_BLK - 1) // MOE_BLK, 0)
    gfirst = (part == 0) & valid
    nxt_item = lambda a: jnp.concatenate([a[1:], jnp.zeros((1,), a.dtype)])
    prv_item = lambda a: jnp.concatenate([jnp.zeros((1,), a.dtype), a[:-1]])
    meta = (grp, r0, n_sub, gfirst, pick(order) % 2, pick(nxt), pick(nxt2), nxt_item(r0), nxt_item(n_sub),
            prv_item(n_sub))
    return tuple(m.astype(I32) for m in meta), starts


def _combine_body(dest0_ref, destn_ref, ys_hbm, gw_ref, sh_ref, x1_ref, gt_ref, gpost_ref, o_ref,
                  ybuf0_ref, ybuf1_ref, sems):
    i = pl.program_id(0)
    last = pl.num_programs(0) - 1
    D = x1_ref.shape[1]
    half = D // 2
    S = half // LANES
    bufs = (ybuf0_ref, ybuf1_ref)

    def gather(dest_ref, slot):
        for t in range(ROW_TILE):
            for k in range(TOP_K):
                _row_copy(ys_hbm, dest_ref[0, 0, t * TOP_K + k], bufs[slot].at[k], t,
                          sems.at[slot], S).start(priority=k % 2)

    def drain(slot):
        for _ in range(ROW_TILE * TOP_K):
            _row_copy(ys_hbm, 0, bufs[slot].at[0], 0, sems.at[slot], S).wait()

    def combine(slot):
        RB = COMBINE_ROWS
        for rb in range(ROW_TILE // RB):
            rows = slice(rb * RB, (rb + 1) * RB)
            wb = [jnp.broadcast_to(gw_ref[rows, k:k + 1], (RB, LANES)) for k in range(TOP_K)]
            ssq = jnp.zeros((RB, 1), F32)
            for s in range(S):
                c_lo = slice(s * LANES, (s + 1) * LANES)
                c_hi = slice(half + s * LANES, half + (s + 1) * LANES)
                fl = sh_ref[rows, c_lo].astype(F32)
                fh = sh_ref[rows, c_hi].astype(F32)
                for k in range(TOP_K):
                    yw = bufs[slot][k, pl.ds(rb * RB * S + s, RB, stride=S), :]
                    yl, yh = _unpack_halves(yw)
                    fl = fl + yl * wb[k]
                    fh = fh + yh * wb[k]
                ssq = ssq + jnp.sum(fl * fl, axis=-1, keepdims=True) + jnp.sum(fh * fh, axis=-1, keepdims=True)
                o_ref[rows, c_lo] = fl
                o_ref[rows, c_hi] = fh
            r = lax.rsqrt(ssq / D + NORM_EPS)
            o_ref[rows, :] = x1_ref[rows, :] + gt_ref[...] * ((o_ref[rows, :] * r) * gpost_ref[...])

    @pl.when(i == 0)
    def _():
        gather(dest0_ref, 0)

    for slot in range(2):
        @pl.when(i % 2 == slot)
        def _():
            drain(slot)
            gather(destn_ref, 1 - slot)
            combine(slot)

            @pl.when(i == last)
            def _():
                drain(1 - slot)


def _combine(dest_tiles, ys, gw, shared, x1, gt, gpost):
    T, D = x1.shape
    half = D // 2
    n_tiles = T // ROW_TILE
    row = lambda w: pl.BlockSpec((ROW_TILE, w), lambda i: (i, 0))
    vec = pl.BlockSpec((1, D), lambda i: (0, 0))
    idx = lambda f: pl.BlockSpec((1, 1, ROW_TILE * TOP_K), f, memory_space=pltpu.SMEM)
    S = half // LANES
    vmem = (2 * _nbytes((TOP_K, ROW_TILE, half), U32) + 4 * _nbytes((ROW_TILE, D), F32)
            + 2 * _nbytes((ROW_TILE, D), BF16) + 8 * _nbytes((ROW_TILE, D), F32) + (4 << 20))
    return pl.pallas_call(
        _combine_body,
        grid=(n_tiles,),
        in_specs=[idx(lambda i: (0, 0, 0)), idx(lambda i: (jnp.minimum(i + 1, n_tiles - 1), 0, 0)),
                  pl.BlockSpec(memory_space=pl.ANY), row(TOP_K), row(D), row(D), vec, vec],
        out_specs=row(D),
        out_shape=jax.ShapeDtypeStruct((T, D), F32),
        scratch_shapes=[pltpu.VMEM((TOP_K, ROW_TILE * S, LANES), U32), pltpu.VMEM((TOP_K, ROW_TILE * S, LANES), U32),
                        pltpu.SemaphoreType.DMA((2,))],
        compiler_params=_cparams(1, vmem),
        name="combine",
    )(dest_tiles, dest_tiles, ys, gw, shared, x1, gt, gpost)


def _layer(x, c, pos_f, w_ada, b_ada, g_mix_pre, g_mix_post, g_ffn_pre, g_ffn_post, w_in, conv_w, ret_gn_w,
           w_out, w_router, router_bias, w_gate, w_up, w_down, ws_gate, ws_up, ws_down):
    T, D = x.shape
    CW = conv_w.shape[-1]
    RW = ret_gn_w.shape[-1]
    assert CW == RW and w_in.shape[1] == 3 * CW + 4 * RW and w_out.shape[0] == CW + RW
    assert T % ROW_TILE == 0 and (T * TOP_K) % (GMM_MAX_SUB * MOE_BLK) == 0 and D % (2 * LANES) == 0
    assert MOE_BLK <= ROW_TILE
    vec = lambda a: a.reshape(1, D)

    mod = _ada(c, w_ada, b_ada)
    sh1, sc1, gt1, sh2, sc2, gt2 = (mod[:, i * D:(i + 1) * D] for i in range(6))

    proj = _inproj(x, vec(g_mix_pre), sc1, sh1, w_in.astype(BF16), tn=CW)
    y_conv = _conv(proj, conv_w, CW)
    y_ret = _retention(proj, pos_f, ret_gn_w.reshape(1, RW), RW, col0_blocks=3)
    x1, h2p, logits_t = _outproj(y_conv, y_ret, w_out.astype(BF16), x, vec(g_mix_post), gt1,
                                 vec(g_ffn_pre), sc2, sh2, w_router)

    eidx_t, gw_t, rank_t, cnt = _route(logits_t, router_bias)
    counts = cnt[:, 0].astype(I32)
    meta, starts = _gmm_meta(counts, T * TOP_K)
    dest = _dest(eidx_t, rank_t, starts, D // 2 // LANES).T
    dest_tiles = dest.reshape(T // ROW_TILE, 1, ROW_TILE * TOP_K)

    xs, shared = _dispatch(dest_tiles, h2p, ws_gate.astype(BF16), ws_up.astype(BF16), ws_down.astype(BF16),
                           D // 2 // LANES)
    ys = _gmm(meta, xs, w_gate, w_up, w_down)
    return _combine(dest_tiles, ys, gw_t.T, shared, x1, gt2, vec(g_ffn_post))


def kernel(x, c, positions, w_ada, b_ada, g_mix_pre, g_mix_post, g_ffn_pre, g_ffn_post, w_in, conv_w, ret_gn_w,
           w_out, w_router, router_bias, w_gate, w_up, w_down, ws_gate, ws_up, ws_down):
    B, S, D = x.shape
    assert B == 1, "one sequence per call"
    xt = x.reshape(S, D)
    pos_f = positions.astype(F32).reshape(S, 1)
    for l in range(w_ada.shape[0]):
        xt = _layer(xt, c, pos_f, w_ada[l], b_ada[l], g_mix_pre[l], g_mix_post[l], g_ffn_pre[l], g_ffn_post[l],
                    w_in[l], conv_w[l], ret_gn_w[l], w_out[l], w_router[l], router_bias[l], w_gate[l], w_up[l],
                    w_down[l], ws_gate[l], ws_up[l], ws_down[l])
    return xt.reshape(B, S, D)
```

```python
import functools
import math

import jax
import jax.numpy as jnp
from jax import lax
from jax.experimental import pallas as pl
from jax.experimental.pallas import tpu as pltpu

F32 = jnp.float32
BF16 = jnp.bfloat16
U32 = jnp.uint32
I32 = jnp.int32

CONV_K = 3
RET_HEAD_DIM = 128
ROPE_THETA = 10000.0
TOP_K = 8
N_GROUPS = 8
TOPK_GROUPS = 4
ROUTED_SCALE = 2.5
NORM_EPS = 1e-6
GN_EPS = 1e-5

LANES = 128
SUBLANES = 8
VMEM_LIMIT_CAP = 56 * 1024 * 1024

RET_CHUNK = 256
MOE_BLK = 128
GMM_MAX_SUB = 4
ROW_TILE = 128
OUTPROJ_SUB = 256
COMBINE_ROWS = 32
INPROJ_PARTS = 4


def _cparams(n_axes, vmem_bytes):
    return pltpu.CompilerParams(
        dimension_semantics=("arbitrary",) * n_axes,
        vmem_limit_bytes=int(min(max(vmem_bytes, 16 * 1024 * 1024), VMEM_LIMIT_CAP)))


def _nbytes(shape, dtype):
    return math.prod(shape) * jnp.dtype(dtype).itemsize


def _dot(a, b):
    return jnp.dot(a, b, preferred_element_type=F32)


def _dot_nt(a, b):
    return lax.dot_general(a, b, (((1,), (1,)), ((), ())), preferred_element_type=F32)


def _dot_tn(a, b):
    return lax.dot_general(a, b, (((0,), (0,)), ((), ())), preferred_element_type=F32)


def _silu(v):
    return v * jax.nn.sigmoid(v)


def _pack_halves(lo, hi):
    lo_b = lax.bitcast_convert_type(lo.astype(BF16).astype(F32), U32)
    hi_b = lax.bitcast_convert_type(hi.astype(BF16).astype(F32), U32)
    return (lo_b >> 16) | (hi_b & jnp.uint32(0xFFFF0000))


def _unpack_halves(w):
    lo = lax.bitcast_convert_type(w << 16, F32)
    hi = lax.bitcast_convert_type(w & jnp.uint32(0xFFFF0000), F32)
    return lo, hi


def _store_row_tiles(ref, val):
    rows, S = val.shape[0], val.shape[1] // LANES
    for s in range(S):
        ref[pl.ds(s, rows, stride=S), :] = val[:, s * LANES:(s + 1) * LANES]


def _load_row_chunk(ref, s, S):
    return ref[pl.ds(s, ref.shape[0] // S, stride=S), :]


def _load_row_tiles(ref, S):
    return jnp.concatenate([_load_row_chunk(ref, s, S) for s in range(S)], axis=1)


def _ada_body(cb_ref, w_ref, b_ref, o_ref):
    s = _silu(cb_ref[...])
    for j in range(o_ref.shape[-1] // LANES):
        sl = slice(j * LANES, (j + 1) * LANES)
        o_ref[:, sl] = jnp.sum(w_ref[:, sl] * s, axis=0, keepdims=True) + b_ref[:, sl]


def _ada(c, w_ada, b_ada):
    D, N = w_ada.shape
    tn = min(1024, N)
    cb = jnp.broadcast_to(c.reshape(D, 1), (D, LANES))
    vmem = 2 * _nbytes((D, tn), F32) + 3 * _nbytes((D, LANES), F32) + (4 << 20)
    return pl.pallas_call(
        _ada_body,
        grid=(N // tn,),
        in_specs=[pl.BlockSpec((D, LANES), lambda j: (0, 0)),
                  pl.BlockSpec((D, tn), lambda j: (0, j)),
                  pl.BlockSpec((1, tn), lambda j: (0, j))],
        out_specs=pl.BlockSpec((1, tn), lambda j: (0, j)),
        out_shape=jax.ShapeDtypeStruct((1, N), F32),
        compiler_params=_cparams(1, vmem),
        name="ada",
    )(cb, w_ada, b_ada.reshape(1, N))


def _inproj_body(x_ref, g_ref, sc_ref, sh_ref, w_ref, o_ref, h_ref):
    tm = x_ref.shape[0]
    part = tm // INPROJ_PARTS

    @pl.when(pl.program_id(1) == 0)
    def _():
        for p in range(INPROJ_PARTS):
            rows = slice(p * part, (p + 1) * part)
            xf = x_ref[rows, :]
            r = xf * lax.rsqrt(jnp.mean(xf * xf, axis=-1, keepdims=True) + NORM_EPS)
            h = ((r * g_ref[...]) * (1.0 + sc_ref[...]) + sh_ref[...]).astype(BF16)
            h_ref[rows, :] = h
            o_ref[rows, :] = _dot(h, w_ref[...]).astype(o_ref.dtype)

    @pl.when(pl.program_id(1) != 0)
    def _():
        o_ref[...] = _dot(h_ref[...], w_ref[...]).astype(o_ref.dtype)


def _inproj(x, g, sc, sh, w_bf, tn):
    T, D = x.shape
    N = w_bf.shape[1]
    tm = min(1024, T)
    vec = pl.BlockSpec((1, D), lambda i, j: (0, 0))
    vmem = (2 * _nbytes((tm, D), F32) + _nbytes((tm, D), BF16) + 2 * _nbytes((D, tn), BF16)
            + 2 * _nbytes((tm, tn), BF16) + 2 * _nbytes((tm, tn), F32) + 2 * _nbytes((tm, D), F32) + (4 << 20))
    return pl.pallas_call(
        _inproj_body,
        grid=(T // tm, N // tn),
        in_specs=[pl.BlockSpec((tm, D), lambda i, j: (i, 0)), vec, vec, vec,
                  pl.BlockSpec((D, tn), lambda i, j: (0, j))],
        out_specs=pl.BlockSpec((tm, tn), lambda i, j: (i, j)),
        out_shape=jax.ShapeDtypeStruct((T, N), BF16),
        scratch_shapes=[pltpu.VMEM((tm, D), BF16)],
        compiler_params=_cparams(2, vmem),
        name="inproj",
    )(x, g, sc, sh, w_bf)


def _conv_body(cb_ref, cc_ref, cx_ref, w_ref, o_ref, u_ref):
    tc = cb_ref.shape[0]

    @pl.when(pl.program_id(1) == 0)
    def _():
        u_ref[0:SUBLANES, :] = jnp.zeros((SUBLANES, u_ref.shape[1]), F32)

    u = cc_ref[...].astype(F32) * cx_ref[...].astype(F32)
    u_ref[SUBLANES:SUBLANES + tc, :] = u
    u1 = u_ref[SUBLANES - 1:SUBLANES - 1 + tc, :]
    u2 = u_ref[SUBLANES - 2:SUBLANES - 2 + tc, :]
    conv = u2 * w_ref[0:1, :] + u1 * w_ref[1:2, :] + u * w_ref[2:3, :]
    o_ref[...] = (cb_ref[...].astype(F32) * conv).astype(o_ref.dtype)
    u_ref[0:SUBLANES, :] = u_ref[tc:tc + SUBLANES, :]


def _conv(proj, conv_w, CW):
    T = proj.shape[0]
    tc = min(512, T)
    tw = min(512, CW)
    nb = CW // tw
    vmem = 8 * _nbytes((tc, tw), BF16) + 8 * _nbytes((tc + SUBLANES, tw), F32) + (4 << 20)
    return pl.pallas_call(
        _conv_body,
        grid=(nb, T // tc),
        in_specs=[pl.BlockSpec((tc, tw), lambda j, i: (i, j)),
                  pl.BlockSpec((tc, tw), lambda j, i: (i, nb + j)),
                  pl.BlockSpec((tc, tw), lambda j, i: (i, 2 * nb + j)),
                  pl.BlockSpec((CONV_K, tw), lambda j, i: (0, j))],
        out_specs=pl.BlockSpec((tc, tw), lambda j, i: (i, j)),
        out_shape=jax.ShapeDtypeStruct((T, CW), BF16),
        scratch_shapes=[pltpu.VMEM((tc + SUBLANES, tw), F32)],
        compiler_params=_cparams(2, vmem),
        name="conv",
    )(proj, proj, proj, conv_w)


def _ret_body(q_ref, k_ref, v_ref, g_ref, pos_ref, inv_ref, gnw_ref, o_ref, state_ref, dec_ref, *, H, C):
    dh = RET_HEAD_DIM
    log_gamma = [math.log1p(-(2.0 ** (-5.0 - h))) for h in range(H)]

    @pl.when(pl.program_id(0) == 0)
    def _():
        state_ref[...] = jnp.zeros_like(state_ref)
        rel = (lax.broadcasted_iota(I32, (C, C), 0) - lax.broadcasted_iota(I32, (C, C), 1)).astype(F32)
        for h in range(H):
            dec_ref[h] = jnp.where(rel >= 0.0, jnp.exp(log_gamma[h] * jnp.maximum(rel, 0.0)), 0.0)

    hc = C // 2
    low = lax.broadcasted_iota(I32, (hc, dh), 1) < dh // 2
    ang = jnp.where(low, pos_ref[0:hc, :], pos_ref[hc:, :]) * inv_ref[...]
    cos_p, sin_p = jnp.cos(ang), jnp.sin(ang)
    cos_r, sin_r = pltpu.roll(cos_p, dh // 2, 1), pltpu.roll(sin_p, dh // 2, 1)
    cos2 = jnp.concatenate([jnp.where(low, cos_p, cos_r), jnp.where(low, cos_r, cos_p)], axis=0)
    sin = jnp.concatenate([jnp.where(low, sin_p, sin_r), jnp.where(low, sin_r, sin_p)], axis=0)
    sin2 = jnp.where(lax.broadcasted_iota(I32, (C, dh), 1) < dh // 2, -sin, sin)
    ridx = lax.broadcasted_iota(I32, (C, 1), 0).astype(F32)

    def rope(t):
        return t * cos2 + pltpu.roll(t, dh // 2, 1) * sin2

    for h in range(H):
        sl = slice(h * dh, (h + 1) * dh)
        lg = log_gamma[h]
        qr = rope(q_ref[:, sl].astype(F32))
        kr = rope(k_ref[:, sl].astype(F32)) * (dh ** -0.5)
        v = v_ref[:, sl]
        q_decay = jnp.exp(lg * (ridx + 1.0))
        k_decay = jnp.exp(lg * (C - 1.0 - ridx))
        scores = _dot_nt(qr.astype(BF16), kr.astype(BF16)) * dec_ref[h]
        state = state_ref[h]
        o = _dot(scores.astype(BF16), v) + _dot((qr * q_decay).astype(BF16), state.astype(BF16))
        state_ref[h] = state * math.exp(lg * C) + _dot_tn((kr * k_decay).astype(BF16), v)
        mu = jnp.mean(o, axis=-1, keepdims=True)
        d = o - mu
        var = jnp.mean(d * d, axis=-1, keepdims=True)
        on = d * lax.rsqrt(var + GN_EPS) * gnw_ref[:, sl]
        o_ref[:, sl] = (_silu(g_ref[:, sl].astype(F32)) * on).astype(o_ref.dtype)


def _retention(proj, pos_f, gn_w, RW, col0_blocks):
    T = proj.shape[0]
    C = min(RET_CHUNK, T)
    H = RW // RET_HEAD_DIM
    half = RET_HEAD_DIM // 2
    inv = ROPE_THETA ** (-jnp.arange(half, dtype=F32) / half)
    inv2 = jnp.concatenate([inv, inv]).reshape(1, RET_HEAD_DIM)
    blk = lambda off: pl.BlockSpec((C, RW), lambda i: (i, col0_blocks + off))
    vmem = (10 * _nbytes((C, RW), BF16) + _nbytes((H, RET_HEAD_DIM, RET_HEAD_DIM), F32)
            + _nbytes((H, C, C), F32) + 24 * _nbytes((C, C), F32) + (8 << 20))
    return pl.pallas_call(
        functools.partial(_ret_body, H=H, C=C),
        grid=(T // C,),
        in_specs=[blk(0), blk(1), blk(2), blk(3),
                  pl.BlockSpec((C, 1), lambda i: (i, 0)),
                  pl.BlockSpec((1, RET_HEAD_DIM), lambda i: (0, 0)),
                  pl.BlockSpec((1, RW), lambda i: (0, 0))],
        out_specs=pl.BlockSpec((C, RW), lambda i: (i, 0)),
        out_shape=jax.ShapeDtypeStruct((T, RW), BF16),
        scratch_shapes=[pltpu.VMEM((H, RET_HEAD_DIM, RET_HEAD_DIM), F32), pltpu.VMEM((H, C, C), F32)],
        compiler_params=_cparams(1, vmem),
        name="ret",
    )(proj, proj, proj, proj, pos_f, inv2, gn_w)


def _outproj_body(yc_ref, yr_ref, w_ref, x_ref, gpost_ref, gt_ref, gpre_ref, sc_ref, sh_ref, wr_ref,
                  x1_ref, h2p_ref, lt_ref, wrhi_ref, wrlo_ref):
    CW = yc_ref.shape[1]
    D = x_ref.shape[1]

    @pl.when(pl.program_id(0) == 0)
    def _():
        w = wr_ref[...]
        hi = w.astype(BF16)
        wrhi_ref[...] = hi
        wrlo_ref[...] = (w - hi.astype(F32)).astype(BF16)

    S = D // 2 // LANES
    tm = x_ref.shape[0]
    sub = min(tm, OUTPROJ_SUB)
    for r0 in range(0, tm, sub):
        rows = slice(r0, r0 + sub)
        m = _dot(yc_ref[rows, :], w_ref[0:CW, :]) + _dot(yr_ref[rows, :], w_ref[CW:, :])
        mn = (m * lax.rsqrt(jnp.mean(m * m, axis=-1, keepdims=True) + NORM_EPS)) * gpost_ref[...]
        x1 = x_ref[rows, :] + gt_ref[...] * mn
        x1_ref[rows, :] = x1
        r = x1 * lax.rsqrt(jnp.mean(x1 * x1, axis=-1, keepdims=True) + NORM_EPS)
        h2 = (r * gpre_ref[...]) * (1.0 + sc_ref[...]) + sh_ref[...]
        hb = h2.astype(BF16)
        _store_row_tiles(h2p_ref.at[pl.ds(r0 * S, sub * S)], _pack_halves(h2[:, :D // 2], h2[:, D // 2:]))
        hl = (h2 - hb.astype(F32)).astype(BF16)
        lt_ref[:, rows] = (_dot_nt(wrhi_ref[...], hb) + _dot_nt(wrhi_ref[...], hl)) + _dot_nt(wrlo_ref[...], hb)


def _outproj(yc, yr, w_out_bf, x, gpost, gt, gpre, sc, sh, w_router):
    T, D = x.shape
    CW, RW = yc.shape[1], yr.shape[1]
    E = w_router.shape[0]
    S = D // 2 // LANES
    tm = min(2 * OUTPROJ_SUB, T)
    vec = pl.BlockSpec((1, D), lambda i: (0, 0))
    row = lambda w: pl.BlockSpec((tm, w), lambda i: (i, 0))
    const = lambda a, b: pl.BlockSpec((a, b), lambda i: (0, 0), pipeline_mode=pl.Buffered(1))
    vmem = (_nbytes((CW + RW, D), BF16) + _nbytes((E, D), F32) + 2 * _nbytes((E, D), BF16)
            + 4 * _nbytes((tm, D), F32) + 4 * _nbytes((tm, D), BF16) + 2 * _nbytes((tm, D // 2), U32)
            + 2 * _nbytes((E, tm), F32) + 5 * _nbytes((tm, D), F32) + (4 << 20))
    return pl.pallas_call(
        _outproj_body,
        grid=(T // tm,),
        in_specs=[row(CW), row(RW), const(CW + RW, D), row(D),
                  vec, vec, vec, vec, vec, const(E, D)],
        out_specs=[row(D), pl.BlockSpec((tm * S, LANES), lambda i: (i, 0)),
                   pl.BlockSpec((E, tm), lambda i: (0, i))],
        out_shape=[jax.ShapeDtypeStruct((T, D), F32),
                   jax.ShapeDtypeStruct((T * S, LANES), U32), jax.ShapeDtypeStruct((E, T), F32)],
        scratch_shapes=[pltpu.VMEM((E, D), BF16), pltpu.VMEM((E, D), BF16)],
        compiler_params=_cparams(1, vmem),
        name="outproj",
    )(yc, yr, w_out_bf, x, gpost, gt, gpre, sc, sh, w_router)


def _route_body(lt_ref, bias_ref, eidx_ref, gw_ref, rank_ref, cnt_ref, carry_ref):
    E, TL = lt_ref.shape
    G, EG = N_GROUPS, E // N_GROUPS
    NEG, BIG = -jnp.inf, 1e9

    @pl.when(pl.program_id(0) == 0)
    def _():
        carry_ref[...] = jnp.zeros_like(carry_ref)

    scores = jax.nn.sigmoid(lt_ref[...]).reshape(G, EG, TL)
    biased = scores + bias_ref[...].reshape(G, EG, 1)
    eio = lax.broadcasted_iota(I32, (G, EG, TL), 1).astype(F32)
    eid = lax.broadcasted_iota(I32, (G, EG, TL), 0).astype(F32) * EG + eio
    gio = lax.broadcasted_iota(I32, (G, 1, TL), 0).astype(F32)

    m1 = jnp.max(biased, axis=1, keepdims=True)
    i1 = jnp.min(jnp.where(biased == m1, eio, BIG), axis=1, keepdims=True)
    m2 = jnp.max(jnp.where(eio == i1, NEG, biased), axis=1, keepdims=True)
    cur = m1 + m2
    gmask = jnp.zeros((G, 1, TL), F32)
    for _ in range(TOPK_GROUPS):
        mm = jnp.max(cur, axis=0, keepdims=True)
        ii = jnp.min(jnp.where(cur == mm, gio, BIG), axis=0, keepdims=True)
        sel = gio == ii
        gmask = jnp.where(sel, 1.0, gmask)
        cur = jnp.where(sel, NEG, cur)

    cur = jnp.where(gmask > 0.0, biased, NEG)
    sels, idxs, ws = [], [], []
    for _ in range(TOP_K):
        mm = jnp.max(jnp.max(cur, axis=0, keepdims=True), axis=1, keepdims=True)
        cand = jnp.where(cur == mm, eid, BIG)
        ii = jnp.min(jnp.min(cand, axis=0, keepdims=True), axis=1, keepdims=True)
        sel = eid == ii
        w = jnp.where(sel, scores, 0.0)
        ws.append(jnp.sum(jnp.sum(w, axis=0, keepdims=True), axis=1, keepdims=True).reshape(1, TL))
        idxs.append(ii.reshape(1, TL))
        sels.append(sel)
        cur = jnp.where(sel, NEG, cur)

    wsum = ws[0]
    for w in ws[1:]:
        wsum = wsum + w
    for k in range(TOP_K):
        eidx_ref[k:k + 1, :] = idxs[k].astype(I32)
        gw_ref[k:k + 1, :] = ws[k] / wsum * ROUTED_SCALE

    chosen = jnp.zeros((G, EG, TL), F32)
    for sel in sels:
        chosen = jnp.where(sel, 1.0, chosen)
    chosen2 = chosen.reshape(E, TL).astype(BF16)
    before = (lax.broadcasted_iota(I32, (TL, TL), 0) < lax.broadcasted_iota(I32, (TL, TL), 1))
    prefix = _dot(chosen2, jnp.where(before, 1.0, 0.0).astype(BF16))
    total = _dot(chosen2, jnp.ones((TL, TL), BF16))
    carry = carry_ref[...]
    rank_full = (carry + prefix).reshape(G, EG, TL)
    for k in range(TOP_K):
        r = jnp.where(sels[k], rank_full, 0.0)
        r = jnp.sum(jnp.sum(r, axis=0, keepdims=True), axis=1, keepdims=True).reshape(1, TL)
        rank_ref[k:k + 1, :] = r.astype(I32)
    carry = carry + total
    carry_ref[...] = carry
    cnt_ref[...] = carry[:, 0:LANES]


def _route(logits_t, bias):
    E, T = logits_t.shape
    TL = min(256, T)
    tok = pl.BlockSpec((TOP_K, TL), lambda i: (0, i))
    vmem = 64 * _nbytes((E, TL), F32) + (8 << 20)
    return pl.pallas_call(
        _route_body,
        grid=(T // TL,),
        in_specs=[pl.BlockSpec((E, TL), lambda i: (0, i)), pl.BlockSpec((E, 1), lambda i: (0, 0))],
        out_specs=[tok, tok, tok, pl.BlockSpec((E, LANES), lambda i: (0, 0))],
        out_shape=[jax.ShapeDtypeStruct((TOP_K, T), I32), jax.ShapeDtypeStruct((TOP_K, T), F32),
                   jax.ShapeDtypeStruct((TOP_K, T), I32), jax.ShapeDtypeStruct((E, LANES), F32)],
        scratch_shapes=[pltpu.VMEM((E, TL), F32)],
        compiler_params=_cparams(1, vmem),
        name="route",
    )(logits_t, bias.reshape(E, 1))


def _dest_body(eidx_ref, rank_ref, starts_ref, dest_ref, *, S):
    E, TL = starts_ref.shape[0], eidx_ref.shape[1]
    eid = lax.broadcasted_iota(I32, (E, TL), 0)
    starts = starts_ref[...]
    for k in range(TOP_K):
        base = jnp.sum(jnp.where(eid == eidx_ref[k:k + 1, :], starts, 0.0), axis=0, keepdims=True)
        dest_ref[k:k + 1, :] = (base.astype(I32) + rank_ref[k:k + 1, :]) * S


def _dest(eidx_t, rank_t, starts, S):
    K, T = eidx_t.shape
    E = starts.shape[0]
    TL = min(512, T)
    tok = pl.BlockSpec((K, TL), lambda i: (0, i))
    return pl.pallas_call(
        functools.partial(_dest_body, S=S),
        grid=(T // TL,),
        in_specs=[tok, tok, pl.BlockSpec((E, 1), lambda i: (0, 0))],
        out_specs=tok,
        out_shape=jax.ShapeDtypeStruct((K, T), I32),
        compiler_params=_cparams(1, 16 * _nbytes((E, TL), F32) + (4 << 20)),
        name="dest",
    )(eidx_t, rank_t, starts.astype(F32).reshape(E, 1))


def _row_span(r, S):
    if isinstance(r, int):
        return pl.ds(r * S, S)
    return pl.ds(pl.multiple_of(r, S), S)


def _row_copy(src, s, dst, d, sem, S):
    return pltpu.make_async_copy(src.at[_row_span(s, S)], dst.at[_row_span(d, S)], sem)


def _dispatch_body(dest_ref, h_hbm, wg_ref, wu_ref, wd_ref, xs_hbm, sh_ref, hbuf_ref, load_sems, sems, *, S):
    i = pl.program_id(0)
    last = pl.num_programs(0) - 1
    slot = i % 2
    tile_rows = ROW_TILE * S

    def load(tile, dst_slot):
        src = h_hbm.at[pl.ds(pl.multiple_of(tile * tile_rows, tile_rows), tile_rows)]
        return pltpu.make_async_copy(src, hbuf_ref.at[dst_slot], load_sems.at[dst_slot])

    def drain(s):
        for _ in range(ROW_TILE * TOP_K):
            _row_copy(hbuf_ref.at[s], 0, xs_hbm, 0, sems.at[s], S).wait()

    @pl.when(i == 0)
    def _():
        load(0, 0).start()
        pad_rows = MOE_BLK * S
        hbuf_ref[1, pl.ds(0, pad_rows), :] = jnp.zeros((pad_rows, LANES), U32)
        pad = pltpu.make_async_copy(hbuf_ref.at[1, pl.ds(0, pad_rows)],
                                    xs_hbm.at[pl.ds(xs_hbm.shape[0] - pad_rows, pad_rows)], sems.at[1])
        pad.start()
        pad.wait()

    @pl.when(i > 0)
    def _():
        drain(1 - slot)

    @pl.when(i < last)
    def _():
        load(i + 1, 1 - slot).start()

    load(i, slot).wait()
    for t in range(ROW_TILE):
        for k in range(TOP_K):
            _row_copy(hbuf_ref.at[slot], t, xs_hbm, dest_ref[0, 0, t * TOP_K + k], sems.at[slot],
                      S).start(priority=k % 2)

    half = S * LANES
    xl, xh = _unpack_halves(_load_row_tiles(hbuf_ref.at[slot], S))
    xl, xh = xl.astype(BF16), xh.astype(BF16)
    gate = _dot(xl, wg_ref[0:half, :]) + _dot(xh, wg_ref[half:, :])
    up = _dot(xl, wu_ref[0:half, :]) + _dot(xh, wu_ref[half:, :])
    sh_ref[...] = _dot((_silu(gate) * up).astype(BF16), wd_ref[...]).astype(sh_ref.dtype)

    @pl.when(i == last)
    def _():
        drain(slot)


def _dispatch(dest_tiles, h2p, wg_bf, wu_bf, wd_bf, S):
    T = h2p.shape[0] // S
    D, SF = wg_bf.shape
    n_tiles = T // ROW_TILE
    const = lambda a, b: pl.BlockSpec((a, b), lambda i: (0, 0), pipeline_mode=pl.Buffered(1))
    vmem = (3 * _nbytes((D, SF), BF16) + 2 * _nbytes((ROW_TILE, D // 2), U32) + 2 * _nbytes((ROW_TILE, D), BF16)
            + 6 * _nbytes((ROW_TILE, D), F32) + (4 << 20))
    return pl.pallas_call(
        functools.partial(_dispatch_body, S=S),
        grid=(n_tiles,),
        in_specs=[pl.BlockSpec((1, 1, ROW_TILE * TOP_K), lambda i: (i, 0, 0), memory_space=pltpu.SMEM),
                  pl.BlockSpec(memory_space=pl.ANY), const(D, SF), const(D, SF), const(SF, D)],
        out_specs=[pl.BlockSpec(memory_space=pl.ANY), pl.BlockSpec((ROW_TILE, D), lambda i: (i, 0))],
        out_shape=[jax.ShapeDtypeStruct(((T * TOP_K + MOE_BLK) * S, LANES), U32),
                   jax.ShapeDtypeStruct((T, D), BF16)],
        scratch_shapes=[pltpu.VMEM((2, ROW_TILE * S, LANES), U32), pltpu.SemaphoreType.DMA((2,)),
                        pltpu.SemaphoreType.DMA((2,))],
        compiler_params=_cparams(1, vmem),
        name="dispatch",
    )(dest_tiles, h2p, wg_bf, wu_bf, wd_bf)


def _gmm_body(grp_ref, r0_ref, nb_ref, gfirst_ref, gslot_ref, gnext_ref, gnext2_ref, r0n_ref, nbn_ref, nbp_ref,
              xs_hbm, wg_hbm, wu_hbm, wd_hbm, ys_hbm,
              wgf_ref, wuf_ref, wdf_ref, wgb_ref, wub_ref, wdb_ref, xbuf_ref, ybuf_ref, sems, xsems, ysems):
    v = pl.program_id(0)
    last = pl.num_programs(0) - 1
    half = wgb_ref.shape[0] // 2
    S = half // LANES
    sub = MOE_BLK * S
    slot = v % 2
    nb = nb_ref[v]
    landing = ((wg_hbm, wgf_ref, wgb_ref), (wu_hbm, wuf_ref, wub_ref), (wd_hbm, wdf_ref, wdb_ref))

    def sub_rows(r0, b):
        first = (r0 + b * MOE_BLK) * S
        return pl.ds(first if isinstance(r0, int) else pl.multiple_of(first, S), sub)

    def x_copy(r0, b, s):
        return pltpu.make_async_copy(xs_hbm.at[sub_rows(r0, b)], xbuf_ref.at[s, pl.ds(b * sub, sub)], xsems.at[s])

    def y_copy(r0, b, s):
        return pltpu.make_async_copy(ybuf_ref.at[s, pl.ds(b * sub, sub)], ys_hbm.at[sub_rows(r0, b)], ysems.at[s])

    def for_subs(n, fn):
        for b in range(GMM_MAX_SUB):
            @pl.when(b < n)
            def _():
                fn(b)

    def fetch(j, g, slot):
        return pltpu.make_async_copy(landing[j][0].at[g], landing[j][1].at[slot], sems.at[slot, j])

    @pl.when(v == 0)
    def _():
        for j in range(3):
            fetch(j, grp_ref[0], 0).start()

        @pl.when(gnext_ref[0] >= 0)
        def _():
            for j in range(3):
                fetch(j, gnext_ref[0], 1).start()

        for_subs(nb, lambda b: x_copy(r0_ref[0], b, 0).start())

    @pl.when(gfirst_ref[v] == 1)
    def _():
        slot = gslot_ref[v]
        nxt2 = gnext2_ref[v]
        for j in range(3):
            fetch(j, grp_ref[v], slot).wait()
            landing[j][2][...] = landing[j][1][slot].astype(BF16)

            @pl.when(nxt2 >= 0)
            def _():
                fetch(j, nxt2, slot).start()

    for_subs(nb, lambda b: x_copy(r0_ref[v], b, slot).wait())
    for_subs(nbn_ref[v], lambda b: x_copy(r0n_ref[v], b, 1 - slot).start())

    def compute(n_sub):
        M = n_sub * sub
        xl, xh = _unpack_halves(_load_row_tiles(xbuf_ref.at[slot, pl.ds(0, M)], S))
        xl, xh = xl.astype(BF16), xh.astype(BF16)
        gate = _dot(xl, wgb_ref[0:half, :]) + _dot(xh, wgb_ref[half:, :])
        up = _dot(xl, wub_ref[0:half, :]) + _dot(xh, wub_ref[half:, :])
        act = (_silu(gate) * up).astype(BF16)
        y = _dot(act, wdb_ref[...])
        _store_row_tiles(ybuf_ref.at[slot, pl.ds(0, M)], _pack_halves(y[:, :half], y[:, half:]))

    for n_sub in range(1, GMM_MAX_SUB + 1):
        pl.when(nb == n_sub)(functools.partial(compute, n_sub))

    for_subs(nbp_ref[v], lambda b: y_copy(0, b, 1 - slot).wait())
    for_subs(nb, lambda b: y_copy(r0_ref[v], b, slot).start())

    @pl.when(v == last)
    def _():
        for_subs(nb, lambda b: y_copy(0, b, slot).wait())
        ybuf_ref[0, pl.ds(0, sub), :] = jnp.zeros((sub, LANES), U32)
        pad = y_copy(ys_hbm.shape[0] // S - MOE_BLK, 0, 0)
        pad.start()
        pad.wait()


def _gmm(meta, xs, w_gate, w_up, w_down):
    E, D, FF = w_gate.shape
    half = D // 2
    S = half // LANES
    NI = meta[0].shape[0]
    n_meta = len(meta)
    hbm = pl.BlockSpec(memory_space=pl.ANY)
    item_rows = GMM_MAX_SUB * MOE_BLK
    vmem = (6 * _nbytes((D, FF), F32) + 3 * _nbytes((D, FF), BF16) + 4 * _nbytes((item_rows, half), U32)
            + 4 * _nbytes((item_rows, D), F32) + (6 << 20))
    return pl.pallas_call(
        _gmm_body,
        grid_spec=pltpu.PrefetchScalarGridSpec(
            num_scalar_prefetch=n_meta,
            grid=(NI,),
            in_specs=[hbm, hbm, hbm, hbm],
            out_specs=hbm,
            scratch_shapes=[pltpu.VMEM((2, D, FF), F32), pltpu.VMEM((2, D, FF), F32), pltpu.VMEM((2, FF, D), F32),
                            pltpu.VMEM((D, FF), BF16), pltpu.VMEM((D, FF), BF16), pltpu.VMEM((FF, D), BF16),
                            pltpu.VMEM((2, item_rows * S, LANES), U32), pltpu.VMEM((2, item_rows * S, LANES), U32),
                            pltpu.SemaphoreType.DMA((2, 3)), pltpu.SemaphoreType.DMA((2,)),
                            pltpu.SemaphoreType.DMA((2,))]),
        out_shape=jax.ShapeDtypeStruct(xs.shape, U32),
        compiler_params=_cparams(1, vmem),
        name="gmm",
    )(*meta, xs, w_gate, w_up, w_down)


def _gmm_meta(counts, n_rows):
    E = counts.shape[0]
    item_rows = GMM_MAX_SUB * MOE_BLK
    NV = E + n_rows // item_rows
    eids = jnp.arange(E, dtype=I32)
    ends = jnp.cumsum(counts)
    starts = ends - counts
    n_vis = (counts + item_rows - 1) // item_rows
    v_end = jnp.cumsum(n_vis)
    v_start = v_end - n_vis
    total = v_end[-1]
    later = jnp.where((eids[None, :] > eids[:, None]) & (counts[None, :] > 0), eids[None, :], E)
    nxt = jnp.min(later, axis=1)
    nxt2 = jnp.min(jnp.where(eids[None, :] > nxt[:, None], later, E), axis=1)
    nxt = jnp.where(nxt == E, -1, nxt)
    nxt2 = jnp.where(nxt2 == E, -1, nxt2)
    order = jnp.cumsum((counts > 0).astype(I32)) - 1
    v = jnp.arange(NV, dtype=I32)
    vc = jnp.minimum(v, total - 1)
    grp = jnp.sum((v_end[None, :] <= vc[:, None]).astype(I32), axis=1)
    onehot = grp[:, None] == eids[None, :]
    pick = lambda a: jnp.sum(jnp.where(onehot, a[None, :], 0), axis=1)
    valid = v < total
    part = vc - pick(v_start)
    r0 = pick(starts) + part * item_rows
    n_sub = jnp.where(valid, (jnp.minimum(pick(ends) - r0, item_rows) + MOE_BLK - 1) // MOE_BLK, 0)
    gfirst = (part == 0) & valid
    nxt_item = lambda a: jnp.concatenate([a[1:], jnp.zeros((1,), a.dtype)])
    prv_item = lambda a: jnp.concatenate([jnp.zeros((1,), a.dtype), a[:-1]])
    meta = (grp, r0, n_sub, gfirst, pick(order) % 2, pick(nxt), pick(nxt2), nxt_item(r0), nxt_item(n_sub),
            prv_item(n_sub))
    return tuple(m.astype(I32) for m in meta), starts


def _combine_body(dest0_ref, destn_ref, ys_hbm, gw_ref, sh_ref, x1_ref, gt_ref, gpost_ref, o_ref,
                  ybuf0_ref, ybuf1_ref, sems):
    i = pl.program_id(0)
    last = pl.num_programs(0) - 1
    D = x1_ref.shape[1]
    half = D // 2
    S = half // LANES
    bufs = (ybuf0_ref, ybuf1_ref)

    def gather(dest_ref, slot):
        for t in range(ROW_TILE):
            for k in range(TOP_K):
                _row_copy(ys_hbm, dest_ref[0, 0, t * TOP_K + k], bufs[slot].at[k], t,
                          sems.at[slot], S).start(priority=k % 2)

    def drain(slot):
        for _ in range(ROW_TILE * TOP_K):
            _row_copy(ys_hbm, 0, bufs[slot].at[0], 0, sems.at[slot], S).wait()

    def combine(slot):
        RB = COMBINE_ROWS
        for rb in range(ROW_TILE // RB):
            rows = slice(rb * RB, (rb + 1) * RB)
            wb = [jnp.broadcast_to(gw_ref[rows, k:k + 1], (RB, LANES)) for k in range(TOP_K)]
            ssq = jnp.zeros((RB, 1), F32)
            for s in range(S):
                c_lo = slice(s * LANES, (s + 1) * LANES)
                c_hi = slice(half + s * LANES, half + (s + 1) * LANES)
                fl = sh_ref[rows, c_lo].astype(F32)
                fh = sh_ref[rows, c_hi].astype(F32)
                for k in range(TOP_K):
                    yw = bufs[slot][k, pl.ds(rb * RB * S + s, RB, stride=S), :]
                    yl, yh = _unpack_halves(yw)
                    fl = fl + yl * wb[k]
                    fh = fh + yh * wb[k]
                ssq = ssq + jnp.sum(fl * fl, axis=-1, keepdims=True) + jnp.sum(fh * fh, axis=-1, keepdims=True)
                o_ref[rows, c_lo] = fl
                o_ref[rows, c_hi] = fh
            r = lax.rsqrt(ssq / D + NORM_EPS)
            o_ref[rows, :] = x1_ref[rows, :] + gt_ref[...] * ((o_ref[rows, :] * r) * gpost_ref[...])

    @pl.when(i == 0)
    def _():
        gather(dest0_ref, 0)

    for slot in range(2):
        @pl.when(i % 2 == slot)
        def _():
            drain(slot)
            gather(destn_ref, 1 - slot)
            combine(slot)

            @pl.when(i == last)
            def _():
                drain(1 - slot)


def _combine(dest_tiles, ys, gw, shared, x1, gt, gpost):
    T, D = x1.shape
    half = D // 2
    n_tiles = T // ROW_TILE
    row = lambda w: pl.BlockSpec((ROW_TILE, w), lambda i: (i, 0))
    vec = pl.BlockSpec((1, D), lambda i: (0, 0))
    idx = lambda f: pl.BlockSpec((1, 1, ROW_TILE * TOP_K), f, memory_space=pltpu.SMEM)
    S = half // LANES
    vmem = (2 * _nbytes((TOP_K, ROW_TILE, half), U32) + 4 * _nbytes((ROW_TILE, D), F32)
            + 2 * _nbytes((ROW_TILE, D), BF16) + 8 * _nbytes((ROW_TILE, D), F32) + (4 << 20))
    return pl.pallas_call(
        _combine_body,
        grid=(n_tiles,),
        in_specs=[idx(lambda i: (0, 0, 0)), idx(lambda i: (jnp.minimum(i + 1, n_tiles - 1), 0, 0)),
                  pl.BlockSpec(memory_space=pl.ANY), row(TOP_K), row(D), row(D), vec, vec],
        out_specs=row(D),
        out_shape=jax.ShapeDtypeStruct((T, D), F32),
        scratch_shapes=[pltpu.VMEM((TOP_K, ROW_TILE * S, LANES), U32), pltpu.VMEM((TOP_K, ROW_TILE * S, LANES), U32),
                        pltpu.SemaphoreType.DMA((2,))],
        compiler_params=_cparams(1, vmem),
        name="combine",
    )(dest_tiles, dest_tiles, ys, gw, shared, x1, gt, gpost)


def _layer(x, c, pos_f, w_ada, b_ada, g_mix_pre, g_mix_post, g_ffn_pre, g_ffn_post, w_in, conv_w, ret_gn_w,
           w_out, w_router, router_bias, w_gate, w_up, w_down, ws_gate, ws_up, ws_down):
    T, D = x.shape
    CW = conv_w.shape[-1]
    RW = ret_gn_w.shape[-1]
    assert CW == RW and w_in.shape[1] == 3 * CW + 4 * RW and w_out.shape[0] == CW + RW
    assert T % ROW_TILE == 0 and (T * TOP_K) % (GMM_MAX_SUB * MOE_BLK) == 0 and D % (2 * LANES) == 0
    assert MOE_BLK <= ROW_TILE
    vec = lambda a: a.reshape(1, D)

    mod = _ada(c, w_ada, b_ada)
    sh1, sc1, gt1, sh2, sc2, gt2 = (mod[:, i * D:(i + 1) * D] for i in range(6))

    proj = _inproj(x, vec(g_mix_pre), sc1, sh1, w_in.astype(BF16), tn=CW)
    y_conv = _conv(proj, conv_w, CW)
    y_ret = _retention(proj, pos_f, ret_gn_w.reshape(1, RW), RW, col0_blocks=3)
    x1, h2p, logits_t = _outproj(y_conv, y_ret, w_out.astype(BF16), x, vec(g_mix_post), gt1,
                                 vec(g_ffn_pre), sc2, sh2, w_router)

    eidx_t, gw_t, rank_t, cnt = _route(logits_t, router_bias)
    counts = cnt[:, 0].astype(I32)
    meta, starts = _gmm_meta(counts, T * TOP_K)
    dest = _dest(eidx_t, rank_t, starts, D // 2 // LANES).T
    dest_tiles = dest.reshape(T // ROW_TILE, 1, ROW_TILE * TOP_K)

    xs, shared = _dispatch(dest_tiles, h2p, ws_gate.astype(BF16), ws_up.astype(BF16), ws_down.astype(BF16),
                           D // 2 // LANES)
    ys = _gmm(meta, xs, w_gate, w_up, w_down)
    return _combine(dest_tiles, ys, gw_t.T, shared, x1, gt2, vec(g_ffn_post))


def kernel(x, c, positions, w_ada, b_ada, g_mix_pre, g_mix_post, g_ffn_pre, g_ffn_post, w_in, conv_w, ret_gn_w,
           w_out, w_router, router_bias, w_gate, w_up, w_down, ws_gate, ws_up, ws_down):
    B, S, D = x.shape
    assert B == 1, "one sequence per call"
    xt = x.reshape(S, D)
    pos_f = positions.astype(F32).reshape(S, 1)
    for l in range(w_ada.shape[0]):
        xt = _layer(xt, c, pos_f, w_ada[l], b_ada[l], g_mix_pre[l], g_mix_post[l], g_ffn_pre[l], g_ffn_post[l],
                    w_in[l], conv_w[l], ret_gn_w[l], w_out[l], w_router[l], router_bias[l], w_gate[l], w_up[l],
                    w_down[l], ws_gate[l], ws_up[l], ws_down[l])
    return xt.reshape(B, S, D)
```

```python
import functools
import math

import jax
import jax.numpy as jnp
from jax import lax
from jax.experimental import pallas as pl
from jax.experimental.pallas import tpu as pltpu

F32 = jnp.float32
BF16 = jnp.bfloat16
U32 = jnp.uint32
I32 = jnp.int32

CONV_K = 3
RET_HEAD_DIM = 128
ROPE_THETA = 10000.0
TOP_K = 8
N_GROUPS = 8
TOPK_GROUPS = 4
ROUTED_SCALE = 2.5
NORM_EPS = 1e-6
GN_EPS = 1e-5

LANES = 128
SUBLANES = 8
VMEM_LIMIT_CAP = 56 * 1024 * 1024

RET_CHUNK = 256
MOE_BLK = 128
GMM_MAX_SUB = 4
ROW_TILE = 128
OUTPROJ_SUB = 256
COMBINE_ROWS = 32
INPROJ_PARTS = 4


def _cparams(n_axes, vmem_bytes):
    return pltpu.CompilerParams(
        dimension_semantics=("arbitrary",) * n_axes,
        vmem_limit_bytes=int(min(max(vmem_bytes, 16 * 1024 * 1024), VMEM_LIMIT_CAP)))


def _nbytes(shape, dtype):
    return math.prod(shape) * jnp.dtype(dtype).itemsize


def _dot(a, b):
    return jnp.dot(a, b, preferred_element_type=F32)


def _dot_nt(a, b):
    return lax.dot_general(a, b, (((1,), (1,)), ((), ())), preferred_element_type=F32)


def _dot_tn(a, b):
    return lax.dot_general(a, b, (((0,), (0,)), ((), ())), preferred_element_type=F32)


def _silu(v):
    return v * jax.nn.sigmoid(v)


def _pack_halves(lo, hi):
    lo_b = lax.bitcast_convert_type(lo.astype(BF16).astype(F32), U32)
    hi_b = lax.bitcast_convert_type(hi.astype(BF16).astype(F32), U32)
    return (lo_b >> 16) | (hi_b & jnp.uint32(0xFFFF0000))


def _unpack_halves(w):
    lo = lax.bitcast_convert_type(w << 16, F32)
    hi = lax.bitcast_convert_type(w & jnp.uint32(0xFFFF0000), F32)
    return lo, hi


def _store_row_tiles(ref, val):
    rows, S = val.shape[0], val.shape[1] // LANES
    for s in range(S):
        ref[pl.ds(s, rows, stride=S), :] = val[:, s * LANES:(s + 1) * LANES]


def _load_row_chunk(ref, s, S):
    return ref[pl.ds(s, ref.shape[0] // S, stride=S), :]


def _load_row_tiles(ref, S):
    return jnp.concatenate([_load_row_chunk(ref, s, S) for s in range(S)], axis=1)


def _ada_body(cb_ref, w_ref, b_ref, o_ref):
    s = _silu(cb_ref[...])
    for j in range(o_ref.shape[-1] // LANES):
        sl = slice(j * LANES, (j + 1) * LANES)
        o_ref[:, sl] = jnp.sum(w_ref[:, sl] * s, axis=0, keepdims=True) + b_ref[:, sl]


def _ada(c, w_ada, b_ada):
    D, N = w_ada.shape
    tn = min(1024, N)
    cb = jnp.broadcast_to(c.reshape(D, 1), (D, LANES))
    vmem = 2 * _nbytes((D, tn), F32) + 3 * _nbytes((D, LANES), F32) + (4 << 20)
    return pl.pallas_call(
        _ada_body,
        grid=(N // tn,),
        in_specs=[pl.BlockSpec((D, LANES), lambda j: (0, 0)),
                  pl.BlockSpec((D, tn), lambda j: (0, j)),
                  pl.BlockSpec((1, tn), lambda j: (0, j))],
        out_specs=pl.BlockSpec((1, tn), lambda j: (0, j)),
        out_shape=jax.ShapeDtypeStruct((1, N), F32),
        compiler_params=_cparams(1, vmem),
        name="ada",
    )(cb, w_ada, b_ada.reshape(1, N))


def _inproj_body(x_ref, g_ref, sc_ref, sh_ref, w_ref, o_ref, h_ref):
    tm = x_ref.shape[0]
    part = tm // INPROJ_PARTS

    @pl.when(pl.program_id(1) == 0)
    def _():
        for p in range(INPROJ_PARTS):
            rows = slice(p * part, (p + 1) * part)
            xf = x_ref[rows, :]
            r = xf * lax.rsqrt(jnp.mean(xf * xf, axis=-1, keepdims=True) + NORM_EPS)
            h = ((r * g_ref[...]) * (1.0 + sc_ref[...]) + sh_ref[...]).astype(BF16)
            h_ref[rows, :] = h
            o_ref[rows, :] = _dot(h, w_ref[...]).astype(o_ref.dtype)

    @pl.when(pl.program_id(1) != 0)
    def _():
        o_ref[...] = _dot(h_ref[...], w_ref[...]).astype(o_ref.dtype)


def _inproj(x, g, sc, sh, w_bf, tn):
    T, D = x.shape
    N = w_bf.shape[1]
    tm = min(1024, T)
    vec = pl.BlockSpec((1, D), lambda i, j: (0, 0))
    vmem = (2 * _nbytes((tm, D), F32) + _nbytes((tm, D), BF16) + 2 * _nbytes((D, tn), BF16)
            + 2 * _nbytes((tm, tn), BF16) + 2 * _nbytes((tm, tn), F32) + 2 * _nbytes((tm, D), F32) + (4 << 20))
    return pl.pallas_call(
        _inproj_body,
        grid=(T // tm, N // tn),
        in_specs=[pl.BlockSpec((tm, D), lambda i, j: (i, 0)), vec, vec, vec,
                  pl.BlockSpec((D, tn), lambda i, j: (0, j))],
        out_specs=pl.BlockSpec((tm, tn), lambda i, j: (i, j)),
        out_shape=jax.ShapeDtypeStruct((T, N), BF16),
        scratch_shapes=[pltpu.VMEM((tm, D), BF16)],
        compiler_params=_cparams(2, vmem),
        name="inproj",
    )(x, g, sc, sh, w_bf)


def _conv_body(cb_ref, cc_ref, cx_ref, w_ref, o_ref, u_ref):
    tc = cb_ref.shape[0]

    @pl.when(pl.program_id(1) == 0)
    def _():
        u_ref[0:SUBLANES, :] = jnp.zeros((SUBLANES, u_ref.shape[1]), F32)

    u = cc_ref[...].astype(F32) * cx_ref[...].astype(F32)
    u_ref[SUBLANES:SUBLANES + tc, :] = u
    u1 = u_ref[SUBLANES - 1:SUBLANES - 1 + tc, :]
    u2 = u_ref[SUBLANES - 2:SUBLANES - 2 + tc, :]
    conv = u2 * w_ref[0:1, :] + u1 * w_ref[1:2, :] + u * w_ref[2:3, :]
    o_ref[...] = (cb_ref[...].astype(F32) * conv).astype(o_ref.dtype)
    u_ref[0:SUBLANES, :] = u_ref[tc:tc + SUBLANES, :]


def _conv(proj, conv_w, CW):
    T = proj.shape[0]
    tc = min(512, T)
    tw = min(512, CW)
    nb = CW // tw
    vmem = 8 * _nbytes((tc, tw), BF16) + 8 * _nbytes((tc + SUBLANES, tw), F32) + (4 << 20)
    return pl.pallas_call(
        _conv_body,
        grid=(nb, T // tc),
        in_specs=[pl.BlockSpec((tc, tw), lambda j, i: (i, j)),
                  pl.BlockSpec((tc, tw), lambda j, i: (i, nb + j)),
                  pl.BlockSpec((tc, tw), lambda j, i: (i, 2 * nb + j)),
                  pl.BlockSpec((CONV_K, tw), lambda j, i: (0, j))],
        out_specs=pl.BlockSpec((tc, tw), lambda j, i: (i, j)),
        out_shape=jax.ShapeDtypeStruct((T, CW), BF16),
        scratch_shapes=[pltpu.VMEM((tc + SUBLANES, tw), F32)],
        compiler_params=_cparams(2, vmem),
        name="conv",
    )(proj, proj, proj, conv_w)


def _ret_body(q_ref, k_ref, v_ref, g_ref, pos_ref, inv_ref, gnw_ref, o_ref, state_ref, dec_ref, *, H, C):
    dh = RET_HEAD_DIM
    log_gamma = [math.log1p(-(2.0 ** (-5.0 - h))) for h in range(H)]

    @pl.when(pl.program_id(0) == 0)
    def _():
        state_ref[...] = jnp.zeros_like(state_ref)
        rel = (lax.broadcasted_iota(I32, (C, C), 0) - lax.broadcasted_iota(I32, (C, C), 1)).astype(F32)
        for h in range(H):
            dec_ref[h] = jnp.where(rel >= 0.0, jnp.exp(log_gamma[h] * jnp.maximum(rel, 0.0)), 0.0)

    hc = C // 2
    low = lax.broadcasted_iota(I32, (hc, dh), 1) < dh // 2
    ang = jnp.where(low, pos_ref[0:hc, :], pos_ref[hc:, :]) * inv_ref[...]
    cos_p, sin_p = jnp.cos(ang), jnp.sin(ang)
    cos_r, sin_r = pltpu.roll(cos_p, dh // 2, 1), pltpu.roll(sin_p, dh // 2, 1)
    cos2 = jnp.concatenate([jnp.where(low, cos_p, cos_r), jnp.where(low, cos_r, cos_p)], axis=0)
    sin = jnp.concatenate([jnp.where(low, sin_p, sin_r), jnp.where(low, sin_r, sin_p)], axis=0)
    sin2 = jnp.where(lax.broadcasted_iota(I32, (C, dh), 1) < dh // 2, -sin, sin)
    ridx = lax.broadcasted_iota(I32, (C, 1), 0).astype(F32)

    def rope(t):
        return t * cos2 + pltpu.roll(t, dh // 2, 1) * sin2

    for h in range(H):
        sl = slice(h * dh, (h + 1) * dh)
        lg = log_gamma[h]
        qr = rope(q_ref[:, sl].astype(F32))
        kr = rope(k_ref[:, sl].astype(F32)) * (dh ** -0.5)
        v = v_ref[:, sl]
        q_decay = jnp.exp(lg * (ridx + 1.0))
        k_decay = jnp.exp(lg * (C - 1.0 - ridx))
        scores = _dot_nt(qr.astype(BF16), kr.astype(BF16)) * dec_ref[h]
        state = state_ref[h]
        o = _dot(scores.astype(BF16), v) + _dot((qr * q_decay).astype(BF16), state.astype(BF16))
        state_ref[h] = state * math.exp(lg * C) + _dot_tn((kr * k_decay).astype(BF16), v)
        mu = jnp.mean(o, axis=-1, keepdims=True)
        d = o - mu
        var = jnp.mean(d * d, axis=-1, keepdims=True)
        on = d * lax.rsqrt(var + GN_EPS) * gnw_ref[:, sl]
        o_ref[:, sl] = (_silu(g_ref[:, sl].astype(F32)) * on).astype(o_ref.dtype)


def _retention(proj, pos_f, gn_w, RW, col0_blocks):
    T = proj.shape[0]
    C = min(RET_CHUNK, T)
    H = RW // RET_HEAD_DIM
    half = RET_HEAD_DIM // 2
    inv = ROPE_THETA ** (-jnp.arange(half, dtype=F32) / half)
    inv2 = jnp.concatenate([inv, inv]).reshape(1, RET_HEAD_DIM)
    blk = lambda off: pl.BlockSpec((C, RW), lambda i: (i, col0_blocks + off))
    vmem = (10 * _nbytes((C, RW), BF16) + _nbytes((H, RET_HEAD_DIM, RET_HEAD_DIM), F32)
            + _nbytes((H, C, C), F32) + 24 * _nbytes((C, C), F32) + (8 << 20))
    return pl.pallas_call(
        functools.partial(_ret_body, H=H, C=C),
        grid=(T // C,),
        in_specs=[blk(0), blk(1), blk(2), blk(3),
                  pl.BlockSpec((C, 1), lambda i: (i, 0)),
                  pl.BlockSpec((1, RET_HEAD_DIM), lambda i: (0, 0)),
                  pl.BlockSpec((1, RW), lambda i: (0, 0))],
        out_specs=pl.BlockSpec((C, RW), lambda i: (i, 0)),
        out_shape=jax.ShapeDtypeStruct((T, RW), BF16),
        scratch_shapes=[pltpu.VMEM((H, RET_HEAD_DIM, RET_HEAD_DIM), F32), pltpu.VMEM((H, C, C), F32)],
        compiler_params=_cparams(1, vmem),
        name="ret",
    )(proj, proj, proj, proj, pos_f, inv2, gn_w)


def _outproj_body(yc_ref, yr_ref, w_ref, x_ref, gpost_ref, gt_ref, gpre_ref, sc_ref, sh_ref, wr_ref,
                  x1_ref, h2p_ref, lt_ref, wrhi_ref, wrlo_ref):
    CW = yc_ref.shape[1]
    D = x_ref.shape[1]

    @pl.when(pl.program_id(0) == 0)
    def _():
        w = wr_ref[...]
        hi = w.astype(BF16)
        wrhi_ref[...] = hi
        wrlo_ref[...] = (w - hi.astype(F32)).astype(BF16)

    S = D // 2 // LANES
    tm = x_ref.shape[0]
    sub = min(tm, OUTPROJ_SUB)
    for r0 in range(0, tm, sub):
        rows = slice(r0, r0 + sub)
        m = _dot(yc_ref[rows, :], w_ref[0:CW, :]) + _dot(yr_ref[rows, :], w_ref[CW:, :])
        mn = (m * lax.rsqrt(jnp.mean(m * m, axis=-1, keepdims=True) + NORM_EPS)) * gpost_ref[...]
        x1 = x_ref[rows, :] + gt_ref[...] * mn
        x1_ref[rows, :] = x1
        r = x1 * lax.rsqrt(jnp.mean(x1 * x1, axis=-1, keepdims=True) + NORM_EPS)
        h2 = (r * gpre_ref[...]) * (1.0 + sc_ref[...]) + sh_ref[...]
        hb = h2.astype(BF16)
        _store_row_tiles(h2p_ref.at[pl.ds(r0 * S, sub * S)], _pack_halves(h2[:, :D // 2], h2[:, D // 2:]))
        hl = (h2 - hb.astype(F32)).astype(BF16)
        lt_ref[:, rows] = (_dot_nt(wrhi_ref[...], hb) + _dot_nt(wrhi_ref[...], hl)) + _dot_nt(wrlo_ref[...], hb)


def _outproj(yc, yr, w_out_bf, x, gpost, gt, gpre, sc, sh, w_router):
    T, D = x.shape
    CW, RW = yc.shape[1], yr.shape[1]
    E = w_router.shape[0]
    S = D // 2 // LANES
    tm = min(2 * OUTPROJ_SUB, T)
    vec = pl.BlockSpec((1, D), lambda i: (0, 0))
    row = lambda w: pl.BlockSpec((tm, w), lambda i: (i, 0))
    const = lambda a, b: pl.BlockSpec((a, b), lambda i: (0, 0), pipeline_mode=pl.Buffered(1))
    vmem = (_nbytes((CW + RW, D), BF16) + _nbytes((E, D), F32) + 2 * _nbytes((E, D), BF16)
            + 4 * _nbytes((tm, D), F32) + 4 * _nbytes((tm, D), BF16) + 2 * _nbytes((tm, D // 2), U32)
            + 2 * _nbytes((E, tm), F32) + 5 * _nbytes((tm, D), F32) + (4 << 20))
    return pl.pallas_call(
        _outproj_body,
        grid=(T // tm,),
        in_specs=[row(CW), row(RW), const(CW + RW, D), row(D),
                  vec, vec, vec, vec, vec, const(E, D)],
        out_specs=[row(D), pl.BlockSpec((tm * S, LANES), lambda i: (i, 0)),
                   pl.BlockSpec((E, tm), lambda i: (0, i))],
        out_shape=[jax.ShapeDtypeStruct((T, D), F32),
                   jax.ShapeDtypeStruct((T * S, LANES), U32), jax.ShapeDtypeStruct((E, T), F32)],
        scratch_shapes=[pltpu.VMEM((E, D), BF16), pltpu.VMEM((E, D), BF16)],
        compiler_params=_cparams(1, vmem),
        name="outproj",
    )(yc, yr, w_out_bf, x, gpost, gt, gpre, sc, sh, w_router)


def _route_body(lt_ref, bias_ref, eidx_ref, gw_ref, rank_ref, cnt_ref, carry_ref):
    E, TL = lt_ref.shape
    G, EG = N_GROUPS, E // N_GROUPS
    NEG, BIG = -jnp.inf, 1e9

    @pl.when(pl.program_id(0) == 0)
    def _():
        carry_ref[...] = jnp.zeros_like(carry_ref)

    scores = jax.nn.sigmoid(lt_ref[...]).reshape(G, EG, TL)
    biased = scores + bias_ref[...].reshape(G, EG, 1)
    eio = lax.broadcasted_iota(I32, (G, EG, TL), 1).astype(F32)
    eid = lax.broadcasted_iota(I32, (G, EG, TL), 0).astype(F32) * EG + eio
    gio = lax.broadcasted_iota(I32, (G, 1, TL), 0).astype(F32)

    m1 = jnp.max(biased, axis=1, keepdims=True)
    i1 = jnp.min(jnp.where(biased == m1, eio, BIG), axis=1, keepdims=True)
    m2 = jnp.max(jnp.where(eio == i1, NEG, biased), axis=1, keepdims=True)
    cur = m1 + m2
    gmask = jnp.zeros((G, 1, TL), F32)
    for _ in range(TOPK_GROUPS):
        mm = jnp.max(cur, axis=0, keepdims=True)
        ii = jnp.min(jnp.where(cur == mm, gio, BIG), axis=0, keepdims=True)
        sel = gio == ii
        gmask = jnp.where(sel, 1.0, gmask)
        cur = jnp.where(sel, NEG, cur)

    cur = jnp.where(gmask > 0.0, biased, NEG)
    sels, idxs, ws = [], [], []
    for _ in range(TOP_K):
        mm = jnp.max(jnp.max(cur, axis=0, keepdims=True), axis=1, keepdims=True)
        cand = jnp.where(cur == mm, eid, BIG)
        ii = jnp.min(jnp.min(cand, axis=0, keepdims=True), axis=1, keepdims=True)
        sel = eid == ii
        w = jnp.where(sel, scores, 0.0)
        ws.append(jnp.sum(jnp.sum(w, axis=0, keepdims=True), axis=1, keepdims=True).reshape(1, TL))
        idxs.append(ii.reshape(1, TL))
        sels.append(sel)
        cur = jnp.where(sel, NEG, cur)

    wsum = ws[0]
    for w in ws[1:]:
        wsum = wsum + w
    for k in range(TOP_K):
        eidx_ref[k:k + 1, :] = idxs[k].astype(I32)
        gw_ref[k:k + 1, :] = ws[k] / wsum * ROUTED_SCALE

    chosen = jnp.zeros((G, EG, TL), F32)
    for sel in sels:
        chosen = jnp.where(sel, 1.0, chosen)
    chosen2 = chosen.reshape(E, TL).astype(BF16)
    before = (lax.broadcasted_iota(I32, (TL, TL), 0) < lax.broadcasted_iota(I32, (TL, TL), 1))
    prefix = _dot(chosen2, jnp.where(before, 1.0, 0.0).astype(BF16))
    total = _dot(chosen2, jnp.ones((TL, TL), BF16))
    carry = carry_ref[...]
    rank_full = (carry + prefix).reshape(G, EG, TL)
    for k in range(TOP_K):
        r = jnp.where(sels[k], rank_full, 0.0)
        r = jnp.sum(jnp.sum(r, axis=0, keepdims=True), axis=1, keepdims=True).reshape(1, TL)
        rank_ref[k:k + 1, :] = r.astype(I32)
    carry = carry + total
    carry_ref[...] = carry
    cnt_ref[...] = carry[:, 0:LANES]


def _route(logits_t, bias):
    E, T = logits_t.shape
    TL = min(256, T)
    tok = pl.BlockSpec((TOP_K, TL), lambda i: (0, i))
    vmem = 64 * _nbytes((E, TL), F32) + (8 << 20)
    return pl.pallas_call(
        _route_body,
        grid=(T // TL,),
        in_specs=[pl.BlockSpec((E, TL), lambda i: (0, i)), pl.BlockSpec((E, 1), lambda i: (0, 0))],
        out_specs=[tok, tok, tok, pl.BlockSpec((E, LANES), lambda i: (0, 0))],
        out_shape=[jax.ShapeDtypeStruct((TOP_K, T), I32), jax.ShapeDtypeStruct((TOP_K, T), F32),
                   jax.ShapeDtypeStruct((TOP_K, T), I32), jax.ShapeDtypeStruct((E, LANES), F32)],
        scratch_shapes=[pltpu.VMEM((E, TL), F32)],
        compiler_params=_cparams(1, vmem),
        name="route",
    )(logits_t, bias.reshape(E, 1))


def _dest_body(eidx_ref, rank_ref, starts_ref, dest_ref, *, S):
    E, TL = starts_ref.shape[0], eidx_ref.shape[1]
    eid = lax.broadcasted_iota(I32, (E, TL), 0)
    starts = starts_ref[...]
    for k in range(TOP_K):
        base = jnp.sum(jnp.where(eid == eidx_ref[k:k + 1, :], starts, 0.0), axis=0, keepdims=True)
        dest_ref[k:k + 1, :] = (base.astype(I32) + rank_ref[k:k + 1, :]) * S


def _dest(eidx_t, rank_t, starts, S):
    K, T = eidx_t.shape
    E = starts.shape[0]
    TL = min(512, T)
    tok = pl.BlockSpec((K, TL), lambda i: (0, i))
    return pl.pallas_call(
        functools.partial(_dest_body, S=S),
        grid=(T // TL,),
        in_specs=[tok, tok, pl.BlockSpec((E, 1), lambda i: (0, 0))],
        out_specs=tok,
        out_shape=jax.ShapeDtypeStruct((K, T), I32),
        compiler_params=_cparams(1, 16 * _nbytes((E, TL), F32) + (4 << 20)),
        name="dest",
    )(eidx_t, rank_t, starts.astype(F32).reshape(E, 1))


def _row_span(r, S):
    if isinstance(r, int):
        return pl.ds(r * S, S)
    return pl.ds(pl.multiple_of(r, S), S)


def _row_copy(src, s, dst, d, sem, S):
    return pltpu.make_async_copy(src.at[_row_span(s, S)], dst.at[_row_span(d, S)], sem)


def _dispatch_body(dest_ref, h_hbm, wgf_ref, wuf_ref, wdf_ref, xs_hbm, sh_ref, wg_ref, wu_ref, wd_ref,
                   hbuf_ref, load_sems, sems, *, S):
    i = pl.program_id(0)
    last = pl.num_programs(0) - 1
    slot = i % 2
    tile_rows = ROW_TILE * S

    def load(tile, dst_slot):
        src = h_hbm.at[pl.ds(pl.multiple_of(tile * tile_rows, tile_rows), tile_rows)]
        return pltpu.make_async_copy(src, hbuf_ref.at[dst_slot], load_sems.at[dst_slot])

    def drain(s):
        for _ in range(ROW_TILE * TOP_K):
            _row_copy(hbuf_ref.at[s], 0, xs_hbm, 0, sems.at[s], S).wait()

    @pl.when(i == 0)
    def _():
        load(0, 0).start()
        wg_ref[...] = wgf_ref[...].astype(BF16)
        wu_ref[...] = wuf_ref[...].astype(BF16)
        wd_ref[...] = wdf_ref[...].astype(BF16)
        pad_rows = MOE_BLK * S
        hbuf_ref[1, pl.ds(0, pad_rows), :] = jnp.zeros((pad_rows, LANES), U32)
        pad = pltpu.make_async_copy(hbuf_ref.at[1, pl.ds(0, pad_rows)],
                                    xs_hbm.at[pl.ds(xs_hbm.shape[0] - pad_rows, pad_rows)], sems.at[1])
        pad.start()
        pad.wait()

    @pl.when(i > 0)
    def _():
        drain(1 - slot)

    @pl.when(i < last)
    def _():
        load(i + 1, 1 - slot).start()

    load(i, slot).wait()
    for t in range(ROW_TILE):
        for k in range(TOP_K):
            _row_copy(hbuf_ref.at[slot], t, xs_hbm, dest_ref[0, 0, t * TOP_K + k], sems.at[slot],
                      S).start(priority=k % 2)

    half = S * LANES
    xl, xh = _unpack_halves(_load_row_tiles(hbuf_ref.at[slot], S))
    xl, xh = xl.astype(BF16), xh.astype(BF16)
    gate = _dot(xl, wg_ref[0:half, :]) + _dot(xh, wg_ref[half:, :])
    up = _dot(xl, wu_ref[0:half, :]) + _dot(xh, wu_ref[half:, :])
    sh_ref[...] = _dot((_silu(gate) * up).astype(BF16), wd_ref[...]).astype(sh_ref.dtype)

    @pl.when(i == last)
    def _():
        drain(slot)


def _dispatch(dest_tiles, h2p, ws_gate, ws_up, ws_down, S):
    T = h2p.shape[0] // S
    D, SF = ws_gate.shape
    n_tiles = T // ROW_TILE
    const = lambda a, b: pl.BlockSpec((a, b), lambda i: (0, 0), pipeline_mode=pl.Buffered(1))
    vmem = (3 * _nbytes((D, SF), F32) + 3 * _nbytes((D, SF), BF16) + 2 * _nbytes((ROW_TILE, D // 2), U32)
            + 2 * _nbytes((ROW_TILE, D), BF16) + 6 * _nbytes((ROW_TILE, D), F32) + (4 << 20))
    return pl.pallas_call(
        functools.partial(_dispatch_body, S=S),
        grid=(n_tiles,),
        in_specs=[pl.BlockSpec((1, 1, ROW_TILE * TOP_K), lambda i: (i, 0, 0), memory_space=pltpu.SMEM),
                  pl.BlockSpec(memory_space=pl.ANY), const(D, SF), const(D, SF), const(SF, D)],
        out_specs=[pl.BlockSpec(memory_space=pl.ANY), pl.BlockSpec((ROW_TILE, D), lambda i: (i, 0))],
        out_shape=[jax.ShapeDtypeStruct(((T * TOP_K + MOE_BLK) * S, LANES), U32),
                   jax.ShapeDtypeStruct((T, D), BF16)],
        scratch_shapes=[pltpu.VMEM((D, SF), BF16), pltpu.VMEM((D, SF), BF16), pltpu.VMEM((SF, D), BF16),
                        pltpu.VMEM((2, ROW_TILE * S, LANES), U32), pltpu.SemaphoreType.DMA((2,)),
                        pltpu.SemaphoreType.DMA((2,))],
        compiler_params=_cparams(1, vmem),
        name="dispatch",
    )(dest_tiles, h2p, ws_gate, ws_up, ws_down)


def _gmm_body(grp_ref, r0_ref, nb_ref, gfirst_ref, gslot_ref, gnext_ref, gnext2_ref, r0n_ref, nbn_ref, nbp_ref,
              xs_hbm, wg_hbm, wu_hbm, wd_hbm, ys_hbm,
              wgf_ref, wuf_ref, wdf_ref, wgb_ref, wub_ref, wdb_ref, xbuf_ref, ybuf_ref, sems, xsems, ysems):
    v = pl.program_id(0)
    last = pl.num_programs(0) - 1
    half = wgb_ref.shape[0] // 2
    S = half // LANES
    sub = MOE_BLK * S
    slot = v % 2
    nb = nb_ref[v]
    landing = ((wg_hbm, wgf_ref, wgb_ref), (wu_hbm, wuf_ref, wub_ref), (wd_hbm, wdf_ref, wdb_ref))

    def sub_rows(r0, b):
        first = (r0 + b * MOE_BLK) * S
        return pl.ds(first if isinstance(r0, int) else pl.multiple_of(first, S), sub)

    def x_copy(r0, b, s):
        return pltpu.make_async_copy(xs_hbm.at[sub_rows(r0, b)], xbuf_ref.at[s, pl.ds(b * sub, sub)], xsems.at[s])

    def y_copy(r0, b, s):
        return pltpu.make_async_copy(ybuf_ref.at[s, pl.ds(b * sub, sub)], ys_hbm.at[sub_rows(r0, b)], ysems.at[s])

    def for_subs(n, fn):
        for b in range(GMM_MAX_SUB):
            @pl.when(b < n)
            def _():
                fn(b)

    def fetch(j, g, slot):
        return pltpu.make_async_copy(landing[j][0].at[g], landing[j][1].at[slot], sems.at[slot, j])

    @pl.when(v == 0)
    def _():
        for j in range(3):
            fetch(j, grp_ref[0], 0).start()

        @pl.when(gnext_ref[0] >= 0)
        def _():
            for j in range(3):
                fetch(j, gnext_ref[0], 1).start()

        for_subs(nb, lambda b: x_copy(r0_ref[0], b, 0).start())

    @pl.when(gfirst_ref[v] == 1)
    def _():
        slot = gslot_ref[v]
        nxt2 = gnext2_ref[v]
        for j in range(3):
            fetch(j, grp_ref[v], slot).wait()
            landing[j][2][...] = landing[j][1][slot].astype(BF16)

            @pl.when(nxt2 >= 0)
            def _():
                fetch(j, nxt2, slot).start()

    for_subs(nb, lambda b: x_copy(r0_ref[v], b, slot).wait())
    for_subs(nbn_ref[v], lambda b: x_copy(r0n_ref[v], b, 1 - slot).start())

    def compute(n_sub):
        M = n_sub * sub
        xl, xh = _unpack_halves(_load_row_tiles(xbuf_ref.at[slot, pl.ds(0, M)], S))
        xl, xh = xl.astype(BF16), xh.astype(BF16)
        gate = _dot(xl, wgb_ref[0:half, :]) + _dot(xh, wgb_ref[half:, :])
        up = _dot(xl, wub_ref[0:half, :]) + _dot(xh, wub_ref[half:, :])
        act = (_silu(gate) * up).astype(BF16)
        y = _dot(act, wdb_ref[...])
        _store_row_tiles(ybuf_ref.at[slot, pl.ds(0, M)], _pack_halves(y[:, :half], y[:, half:]))

    for n_sub in range(1, GMM_MAX_SUB + 1):
        pl.when(nb == n_sub)(functools.partial(compute, n_sub))

    for_subs(nbp_ref[v], lambda b: y_copy(0, b, 1 - slot).wait())
    for_subs(nb, lambda b: y_copy(r0_ref[v], b, slot).start())

    @pl.when(v == last)
    def _():
        for_subs(nb, lambda b: y_copy(0, b, slot).wait())
        ybuf_ref[0, pl.ds(0, sub), :] = jnp.zeros((sub, LANES), U32)
        pad = y_copy(ys_hbm.shape[0] // S - MOE_BLK, 0, 0)
        pad.start()
        pad.wait()


def _gmm(meta, xs, w_gate, w_up, w_down):
    E, D, FF = w_gate.shape
    half = D // 2
    S = half // LANES
    NI = meta[0].shape[0]
    n_meta = len(meta)
    hbm = pl.BlockSpec(memory_space=pl.ANY)
    item_rows = GMM_MAX_SUB * MOE_BLK
    vmem = (6 * _nbytes((D, FF), F32) + 3 * _nbytes((D, FF), BF16) + 4 * _nbytes((item_rows, half), U32)
            + 4 * _nbytes((item_rows, D), F32) + (6 << 20))
    return pl.pallas_call(
        _gmm_body,
        grid_spec=pltpu.PrefetchScalarGridSpec(
            num_scalar_prefetch=n_meta,
            grid=(NI,),
            in_specs=[hbm, hbm, hbm, hbm],
            out_specs=hbm,
            scratch_shapes=[pltpu.VMEM((2, D, FF), F32), pltpu.VMEM((2, D, FF), F32), pltpu.VMEM((2, FF, D), F32),
                            pltpu.VMEM((D, FF), BF16), pltpu.VMEM((D, FF), BF16), pltpu.VMEM((FF, D), BF16),
                            pltpu.VMEM((2, item_rows * S, LANES), U32), pltpu.VMEM((2, item_rows * S, LANES), U32),
                            pltpu.SemaphoreType.DMA((2, 3)), pltpu.SemaphoreType.DMA((2,)),
                            pltpu.SemaphoreType.DMA((2,))]),
        out_shape=jax.ShapeDtypeStruct(xs.shape, U32),
        compiler_params=_cparams(1, vmem),
        name="gmm",
    )(*meta, xs, w_gate, w_up, w_down)


def _gmm_meta(counts, n_rows):
    E = counts.shape[0]
    item_rows = GMM_MAX_SUB * MOE_BLK
    NV = E + n_rows // item_rows
    eids = jnp.arange(E, dtype=I32)
    ends = jnp.cumsum(counts)
    starts = ends - counts
    n_vis = (counts + item_rows - 1) // item_rows
    v_end = jnp.cumsum(n_vis)
    v_start = v_end - n_vis
    total = v_end[-1]
    later = jnp.where((eids[None, :] > eids[:, None]) & (counts[None, :] > 0), eids[None, :], E)
    nxt = jnp.min(later, axis=1)
    nxt2 = jnp.min(jnp.where(eids[None, :] > nxt[:, None], later, E), axis=1)
    nxt = jnp.where(nxt == E, -1, nxt)
    nxt2 = jnp.where(nxt2 == E, -1, nxt2)
    order = jnp.cumsum((counts > 0).astype(I32)) - 1
    v = jnp.arange(NV, dtype=I32)
    vc = jnp.minimum(v, total - 1)
    grp = jnp.sum((v_end[None, :] <= vc[:, None]).astype(I32), axis=1)
    onehot = grp[:, None] == eids[None, :]
    pick = lambda a: jnp.sum(jnp.where(onehot, a[None, :], 0), axis=1)
    valid = v < total
    part = vc - pick(v_start)
    r0 = pick(starts) + part * item_rows
    n_sub = jnp.where(valid, (jnp.minimum(pick(ends) - r0, item_rows) + MOE_BLK - 1) // MOE_BLK, 0)
    gfirst = (part == 0) & valid
    nxt_item = lambda a: jnp.concatenate([a[1:], jnp.zeros((1,), a.dtype)])
    prv_item = lambda a: jnp.concatenate([jnp.zeros((1,), a.dtype), a[:-1]])
    meta = (grp, r0, n_sub, gfirst, pick(order) % 2, pick(nxt), pick(nxt2), nxt_item(r0), nxt_item(n_sub),
            prv_item(n_sub))
    return tuple(m.astype(I32) for m in meta), starts


def _combine_body(dest0_ref, destn_ref, ys_hbm, gw_ref, sh_ref, x1_ref, gt_ref, gpost_ref, o_ref,
                  ybuf0_ref, ybuf1_ref, sems):
    i = pl.program_id(0)
    last = pl.num_programs(0) - 1
    D = x1_ref.shape[1]
    half = D // 2
    S = half // LANES
    bufs = (ybuf0_ref, ybuf1_ref)

    def gather(dest_ref, slot):
        for t in range(ROW_TILE):
            for k in range(TOP_K):
                _row_copy(ys_hbm, dest_ref[0, 0, t * TOP_K + k], bufs[slot].at[k], t,
                          sems.at[slot], S).start(priority=k % 2)

    def drain(slot):
        for _ in range(ROW_TILE * TOP_K):
            _row_copy(ys_hbm, 0, bufs[slot].at[0], 0, sems.at[slot], S).wait()

    def combine(slot):
        RB = COMBINE_ROWS
        for rb in range(ROW_TILE // RB):
            rows = slice(rb * RB, (rb + 1) * RB)
            wb = [jnp.broadcast_to(gw_ref[rows, k:k + 1], (RB, LANES)) for k in range(TOP_K)]
            ssq = jnp.zeros((RB, 1), F32)
            for s in range(S):
                c_lo = slice(s * LANES, (s + 1) * LANES)
                c_hi = slice(half + s * LANES, half + (s + 1) * LANES)
                fl = sh_ref[rows, c_lo].astype(F32)
                fh = sh_ref[rows, c_hi].astype(F32)
                for k in range(TOP_K):
                    yw = bufs[slot][k, pl.ds(rb * RB * S + s, RB, stride=S), :]
                    yl, yh = _unpack_halves(yw)
                    fl = fl + yl * wb[k]
                    fh = fh + yh * wb[k]
                ssq = ssq + jnp.sum(fl * fl, axis=-1, keepdims=True) + jnp.sum(fh * fh, axis=-1, keepdims=True)
                o_ref[rows, c_lo] = fl
                o_ref[rows, c_hi] = fh
            r = lax.rsqrt(ssq / D + NORM_EPS)
            o_ref[rows, :] = x1_ref[rows, :] + gt_ref[...] * ((o_ref[rows, :] * r) * gpost_ref[...])

    @pl.when(i == 0)
    def _():
        gather(dest0_ref, 0)

    for slot in range(2):
        @pl.when(i % 2 == slot)
        def _():
            drain(slot)
            gather(destn_ref, 1 - slot)
            combine(slot)

            @pl.when(i == last)
            def _():
                drain(1 - slot)


def _combine(dest_tiles, ys, gw, shared, x1, gt, gpost):
    T, D = x1.shape
    half = D // 2
    n_tiles = T // ROW_TILE
    row = lambda w: pl.BlockSpec((ROW_TILE, w), lambda i: (i, 0))
    vec = pl.BlockSpec((1, D), lambda i: (0, 0))
    idx = lambda f: pl.BlockSpec((1, 1, ROW_TILE * TOP_K), f, memory_space=pltpu.SMEM)
    S = half // LANES
    vmem = (2 * _nbytes((TOP_K, ROW_TILE, half), U32) + 4 * _nbytes((ROW_TILE, D), F32)
            + 2 * _nbytes((ROW_TILE, D), BF16) + 8 * _nbytes((ROW_TILE, D), F32) + (4 << 20))
    return pl.pallas_call(
        _combine_body,
        grid=(n_tiles,),
        in_specs=[idx(lambda i: (0, 0, 0)), idx(lambda i: (jnp.minimum(i + 1, n_tiles - 1), 0, 0)),
                  pl.BlockSpec(memory_space=pl.ANY), row(TOP_K), row(D), row(D), vec, vec],
        out_specs=row(D),
        out_shape=jax.ShapeDtypeStruct((T, D), F32),
        scratch_shapes=[pltpu.VMEM((TOP_K, ROW_TILE * S, LANES), U32), pltpu.VMEM((TOP_K, ROW_TILE * S, LANES), U32),
                        pltpu.SemaphoreType.DMA((2,))],
        compiler_params=_cparams(1, vmem),
        name="combine",
    )(dest_tiles, dest_tiles, ys, gw, shared, x1, gt, gpost)


def _layer(x, c, pos_f, w_ada, b_ada, g_mix_pre, g_mix_post, g_ffn_pre, g_ffn_post, w_in, conv_w, ret_gn_w,
           w_out, w_router, router_bias, w_gate, w_up, w_down, ws_gate, ws_up, ws_down):
    T, D = x.shape
    CW = conv_w.shape[-1]
    RW = ret_gn_w.shape[-1]
    assert CW == RW and w_in.shape[1] == 3 * CW + 4 * RW and w_out.shape[0] == CW + RW
    assert T % ROW_TILE == 0 and (T * TOP_K) % (GMM_MAX_SUB * MOE_BLK) == 0 and D % (2 * LANES) == 0
    assert MOE_BLK <= ROW_TILE
    vec = lambda a: a.reshape(1, D)

    mod = _ada(c, w_ada, b_ada)
    sh1, sc1, gt1, sh2, sc2, gt2 = (mod[:, i * D:(i + 1) * D] for i in range(6))

    proj = _inproj(x, vec(g_mix_pre), sc1, sh1, w_in.astype(BF16), tn=CW)
    y_conv = _conv(proj, conv_w, CW)
    y_ret = _retention(proj, pos_f, ret_gn_w.reshape(1, RW), RW, col0_blocks=3)
    x1, h2p, logits_t = _outproj(y_conv, y_ret, w_out.astype(BF16), x, vec(g_mix_post), gt1,
                                 vec(g_ffn_pre), sc2, sh2, w_router)

    eidx_t, gw_t, rank_t, cnt = _route(logits_t, router_bias)
    counts = cnt[:, 0].astype(I32)
    meta, starts = _gmm_meta(counts, T * TOP_K)
    dest = _dest(eidx_t, rank_t, starts, D // 2 // LANES).T
    dest_tiles = dest.reshape(T // ROW_TILE, 1, ROW_TILE * TOP_K)

    xs, shared = _dispatch(dest_tiles, h2p, ws_gate, ws_up, ws_down, D // 2 // LANES)
    ys = _gmm(meta, xs, w_gate, w_up, w_down)
    return _combine(dest_tiles, ys, gw_t.T, shared, x1, gt2, vec(g_ffn_post))


def kernel(x, c, positions, w_ada, b_ada, g_mix_pre, g_mix_post, g_ffn_pre, g_ffn_post, w_in, conv_w, ret_gn_w,
           w_out, w_router, router_bias, w_gate, w_up, w_down, ws_gate, ws_up, ws_down):
    B, S, D = x.shape
    assert B == 1, "one sequence per call"
    xt = x.reshape(S, D)
    pos_f = positions.astype(F32).reshape(S, 1)
    for l in range(w_ada.shape[0]):
        xt = _layer(xt, c, pos_f, w_ada[l], b_ada[l], g_mix_pre[l], g_mix_post[l], g_ffn_pre[l], g_ffn_post[l],
                    w_in[l], conv_w[l], ret_gn_w[l], w_out[l], w_router[l], router_bias[l], w_gate[l], w_up[l],
                    w_down[l], ws_gate[l], ws_up[l], ws_down[l])
    return xt.reshape(B, S, D)
```
